```python
import math
import jax
import jax.numpy as jnp
from jax import lax
import numpy as np


D_MODEL = 1024
BATCH = 8
SEQ = 4096
DEPTH = 2

CHUNK = 64
N_BRANCH = 4
BRANCH_W = 512
SGU_BLOCK = 128
SGU_GROUPS = 4
SGU_GW = BRANCH_W // SGU_GROUPS
LRU_W = BRANCH_W
LRU_HEADS = 8
LRU_HD = LRU_W // LRU_HEADS
LRU_CONV = 4
LRU_C = 8.0
GDN_HEADS = 4
GDN_DK = 128
GDN_DV = 128
GDN_CONV = 4
POOL_WINDOWS = (2, 4, 8, 16)
POOL_GROUPS = 4
POOL_GW = BRANCH_W // POOL_GROUPS
D_FF = 2816
EPS = 1e-6
IN_SIZES = (BRANCH_W, BRANCH_W, LRU_W, LRU_W, GDN_HEADS * GDN_DK, GDN_HEADS * GDN_DK,
            GDN_HEADS * GDN_DV, GDN_HEADS * GDN_DV, GDN_HEADS, GDN_HEADS, BRANCH_W,
            N_BRANCH * D_MODEL)
P_IN = sum(IN_SIZES)

kernel_name = 'chunk_causal_hybrid_gated_merge'

F32 = jnp.float32


def rms_norm(x, g):
    xf = x.astype(F32)
    y = xf * lax.rsqrt(jnp.mean(xf * xf, axis=-1, keepdims=True) + EPS)
    return (y * g.astype(F32)).astype(x.dtype)


def swiglu(h, wg, wu, wd):
    a = jnp.einsum('bsd,df->bsf', h, wg)
    b = jnp.einsum('bsd,df->bsf', h, wu)
    return jnp.einsum('bsf,fd->bsd', jax.nn.silu(a) * b, wd)


def causal_dwconv(x, w):
    K = w.shape[0]
    S = x.shape[1]
    xp = jnp.pad(x, ((0, 0), (K - 1, 0), (0, 0)))
    y = xp[:, 0:S] * w[0]
    for k in range(1, K):
        y = y + xp[:, k:k + S] * w[k]
    return y


def l2norm(t):
    return t * lax.rsqrt(jnp.sum(t * t, axis=-1, keepdims=True) + EPS)


def sgu_mixer(u, v, ln_g, ln_b, w_s, b_s):
    B, S, _ = u.shape
    u = jax.nn.gelu(u)
    vf = jax.nn.gelu(v).astype(F32)
    mu = jnp.mean(vf, axis=-1, keepdims=True)
    var = jnp.mean(jnp.square(vf - mu), axis=-1, keepdims=True)
    vn = (vf - mu) * lax.rsqrt(var + EPS) * ln_g.astype(F32) + ln_b.astype(F32)
    n = S // SGU_BLOCK
    vb = vn.reshape(B, n, SGU_BLOCK, SGU_GROUPS, SGU_GW)
    pos_chunk = jnp.arange(SGU_BLOCK) // CHUNK
    mask = pos_chunk[:, None] >= pos_chunk[None, :]
    w = jnp.where(mask[None], w_s.astype(F32), 0.0)
    mixed = jnp.einsum('gij,bnjgc->bnigc', w, vb) + b_s.astype(F32).T[None, None, :, :, None]
    return u * mixed.reshape(B, S, BRANCH_W).astype(u.dtype)


def rglru_mixer(xb, gate, conv_w, conv_b, wa, ba, wx, bx, lam):
    B, S, _ = xb.shape
    xc = (causal_dwconv(xb, conv_w) + conv_b).astype(F32)
    xh = xc.reshape(B, S, LRU_HEADS, LRU_HD)
    r = jax.nn.sigmoid(jnp.einsum('bshi,hij->bshj', xh, wa.astype(F32)).reshape(B, S, LRU_W) + ba.astype(F32))
    i = jax.nn.sigmoid(jnp.einsum('bshi,hij->bshj', xh, wx.astype(F32)).reshape(B, S, LRU_W) + bx.astype(F32))
    log_a = -LRU_C * r * jax.nn.softplus(-lam.astype(F32))
    a = jnp.exp(log_a)
    mult = jnp.sqrt(-jnp.expm1(2.0 * log_a))
    b = mult * (i * xc)

    def combine(left, right):
        a1, b1 = left
        a2, b2 = right
        return a1 * a2, a2 * b1 + b2

    _, h = lax.associative_scan(combine, (a, b), axis=1)
    return (h * jax.nn.gelu(gate.astype(F32))).astype(xb.dtype)


def gated_deltanet_mixer(q, k, v, z, beta_pre, alpha_pre, conv_w, a_log, dt_bias, norm_g):
    B, S, _ = q.shape
    H, C = GDN_HEADS, CHUNK
    n = S // C
    qkv = jax.nn.silu(causal_dwconv(jnp.concatenate([q, k, v], axis=-1), conv_w)).astype(F32)
    q, k, v = jnp.split(qkv, [H * GDN_DK, 2 * H * GDN_DK], axis=-1)
    q = l2norm(q.reshape(B, S, H, GDN_DK)) * (GDN_DK ** -0.5)
    k = l2norm(k.reshape(B, S, H, GDN_DK))
    v = v.reshape(B, S, H, GDN_DV)
    beta = jax.nn.sigmoid(beta_pre.astype(F32))
    g = -jnp.exp(a_log.astype(F32)) * jax.nn.softplus(alpha_pre.astype(F32) + dt_bias.astype(F32))

    def chunks(t):
        t = t.reshape((B, n, C, H) + t.shape[3:])
        return jnp.moveaxis(t, 3, 1)

    qc, kc, vc = chunks(q), chunks(k), chunks(v)
    bc, gc = chunks(beta), chunks(g)
    gcum = jnp.cumsum(gc, axis=-1)
    idx = jnp.arange(C)
    incl = idx[:, None] >= idx[None, :]
    strict = idx[:, None] > idx[None, :]
    decay = jnp.exp(jnp.where(incl, gcum[..., :, None] - gcum[..., None, :], -jnp.inf))
    kb = kc * bc[..., None]
    a_mat = jnp.where(strict, jnp.einsum('bhnid,bhnjd->bhnij', kb, kc) * decay, 0.0)
    eye = jnp.eye(C, dtype=F32)
    t_mat = lax.linalg.triangular_solve(eye + a_mat, jnp.broadcast_to(eye, a_mat.shape),
                                        left_side=True, lower=True, unit_diagonal=True)
    u = jnp.einsum('bhnij,bhnjd->bhnid', t_mat, vc * bc[..., None])
    w = jnp.einsum('bhnij,bhnjd->bhnid', t_mat, kb * jnp.exp(gcum)[..., None])
    attn = jnp.where(incl, jnp.einsum('bhnid,bhnjd->bhnij', qc, kc) * decay, 0.0)

    def step(state, inp):
        q_i, k_i, u_i, w_i, g_i, attn_i = inp
        v_new = u_i - jnp.einsum('bhcd,bhde->bhce', w_i, state)
        o = (jnp.einsum('bhcd,bhde->bhce', q_i * jnp.exp(g_i)[..., None], state)
             + jnp.einsum('bhij,bhje->bhie', attn_i, v_new))
        g_last = g_i[..., -1]
        state = (state * jnp.exp(g_last)[..., None, None]
                 + jnp.einsum('bhcd,bhce->bhde', k_i * jnp.exp(g_last[..., None] - g_i)[..., None], v_new))
        return state, o

    xs = tuple(jnp.moveaxis(t, 2, 0) for t in (qc, kc, u, w, gcum, attn))
    state0 = jnp.zeros((B, H, GDN_DK, GDN_DV), F32)
    _, o = lax.scan(step, state0, xs)
    o = jnp.transpose(o, (1, 0, 3, 2, 4)).reshape(B, S, H, GDN_DV)
    o = o * lax.rsqrt(jnp.mean(o * o, axis=-1, keepdims=True) + EPS) * norm_g.astype(F32)
    o = o * jax.nn.silu(z.astype(F32).reshape(B, S, H, GDN_DV))
    return o.reshape(B, S, H * GDN_DV).astype(z.dtype)


def pool_mixer(xd, w_pool, scale):
    B, S, _ = xd.shape
    xf = xd.astype(F32).reshape(B, S, POOL_GROUPS, POOL_GW)
    cs = jnp.cumsum(xf, axis=1)
    t = jnp.arange(S)
    outs = []
    for gi, win in enumerate(POOL_WINDOWS):
        c = cs[:, :, gi]
        prev = jnp.pad(c, ((0, 0), (win, 0), (0, 0)))[:, :S]
        cnt = jnp.minimum(t + 1, win).astype(F32)[None, :, None]
        outs.append((c - prev) / cnt - xf[:, :, gi])
    pooled = jnp.stack(outs, axis=2)
    y = jnp.einsum('bsgc,gcd->bsgd', pooled, w_pool.astype(F32)).reshape(B, S, BRANCH_W)
    return (y * scale.astype(F32)).astype(xd.dtype)


def setup_inputs(seed: int = 0) -> dict:
    key = jax.random.key(seed)
    ks = iter(jax.random.split(key, 40))
    L, D = DEPTH, D_MODEL

    def nrm(shape, scale):
        return jax.random.normal(next(ks), shape, F32) * scale

    def gain(shape):
        return 1.0 + 0.02 * jax.random.normal(next(ks), shape, F32)

    x = jax.random.normal(next(ks), (BATCH, SEQ, D), F32)
    ff1_norm = gain((L, D))
    ff1_wg = nrm((L, D, D_FF), D ** -0.5)
    ff1_wu = nrm((L, D, D_FF), D ** -0.5)
    ff1_wd = nrm((L, D_FF, D), D_FF ** -0.5)
    mix_norm = gain((L, D))
    w_in = nrm((L, D, P_IN), D ** -0.5)
    sgu_ln_g = gain((L, BRANCH_W))
    sgu_ln_b = nrm((L, BRANCH_W), 0.02)
    sgu_w = nrm((L, SGU_GROUPS, SGU_BLOCK, SGU_BLOCK), 0.5 * SGU_BLOCK ** -0.5)
    sgu_b = 1.0 + nrm((L, SGU_GROUPS, SGU_BLOCK), 0.1)
    lru_conv_w = nrm((L, LRU_CONV, LRU_W), LRU_CONV ** -0.5)
    lru_conv_b = nrm((L, LRU_W), 0.02)
    lru_wa = nrm((L, LRU_HEADS, LRU_HD, LRU_HD), LRU_HD ** -0.5)
    lru_ba = nrm((L, LRU_W), 0.02)
    lru_wx = nrm((L, LRU_HEADS, LRU_HD, LRU_HD), LRU_HD ** -0.5)
    lru_bx = nrm((L, LRU_W), 0.02)
    a_c = jax.random.uniform(next(ks), (L, LRU_W), F32, minval=0.9, maxval=0.999)
    a_base = a_c ** (1.0 / LRU_C)
    lru_lambda = jnp.log(a_base) - jnp.log1p(-a_base)
    gdn_conv_w = nrm((L, GDN_CONV, GDN_HEADS * (2 * GDN_DK + GDN_DV)), GDN_CONV ** -0.5)
    gdn_a_log = jnp.log(jax.random.uniform(next(ks), (L, GDN_HEADS), F32, minval=1.0, maxval=16.0))
    lo, hi = math.log(1e-3), math.log(1e-1)
    dt = jnp.exp(jax.random.uniform(next(ks), (L, GDN_HEADS), F32) * (hi - lo) + lo)
    gdn_dt_bias = dt + jnp.log(-jnp.expm1(-dt))
    gdn_norm_g = gain((L, GDN_DV))
    pool_w = nrm((L, POOL_GROUPS, POOL_GW, POOL_GW), POOL_GW ** -0.5)
    pool_scale = 1.0 + nrm((L, BRANCH_W), 0.1)
    w_branch = nrm((L, N_BRANCH, BRANCH_W, D), BRANCH_W ** -0.5)
    w_out = nrm((L, D, D), D ** -0.5)
    ff2_norm = gain((L, D))
    ff2_wg = nrm((L, D, D_FF), D ** -0.5)
    ff2_wu = nrm((L, D, D_FF), D ** -0.5)
    ff2_wd = nrm((L, D_FF, D), D_FF ** -0.5)
    final_norm = gain((D,))
    return {'x': x, 'ff1_norm': ff1_norm, 'ff1_wg': ff1_wg, 'ff1_wu': ff1_wu, 'ff1_wd': ff1_wd,
            'mix_norm': mix_norm, 'w_in': w_in, 'sgu_ln_g': sgu_ln_g, 'sgu_ln_b': sgu_ln_b,
            'sgu_w': sgu_w, 'sgu_b': sgu_b, 'lru_conv_w': lru_conv_w, 'lru_conv_b': lru_conv_b,
            'lru_wa': lru_wa, 'lru_ba': lru_ba, 'lru_wx': lru_wx, 'lru_bx': lru_bx,
            'lru_lambda': lru_lambda, 'gdn_conv_w': gdn_conv_w, 'gdn_a_log': gdn_a_log,
            'gdn_dt_bias': gdn_dt_bias, 'gdn_norm_g': gdn_norm_g, 'pool_w': pool_w,
            'pool_scale': pool_scale, 'w_branch': w_branch, 'w_out': w_out,
            'ff2_norm': ff2_norm, 'ff2_wg': ff2_wg, 'ff2_wu': ff2_wu, 'ff2_wd': ff2_wd,
            'final_norm': final_norm}


def reference(x, ff1_norm, ff1_wg, ff1_wu, ff1_wd, mix_norm, w_in, sgu_ln_g, sgu_ln_b,
              sgu_w, sgu_b, lru_conv_w, lru_conv_b, lru_wa, lru_ba, lru_wx, lru_bx,
              lru_lambda, gdn_conv_w, gdn_a_log, gdn_dt_bias, gdn_norm_g, pool_w,
              pool_scale, w_branch, w_out, ff2_norm, ff2_wg, ff2_wu, ff2_wd, final_norm):
    B, S, _ = x.shape
    split_at = np.cumsum(IN_SIZES)[:-1].tolist()
    for l in range(DEPTH):
        x = x + 0.5 * swiglu(rms_norm(x, ff1_norm[l]), ff1_wg[l], ff1_wu[l], ff1_wd[l])
        h = rms_norm(x, mix_norm[l])
        proj = jnp.einsum('bsd,dp->bsp', h, w_in[l])
        (a_u, a_v, b_x, b_g, c_q, c_k, c_v, c_z, c_beta, c_alpha, d_x,
         gate_pre) = jnp.split(proj, split_at, axis=-1)
        y_a = sgu_mixer(a_u, a_v, sgu_ln_g[l], sgu_ln_b[l], sgu_w[l], sgu_b[l])
        y_b = rglru_mixer(b_x, b_g, lru_conv_w[l], lru_conv_b[l], lru_wa[l], lru_ba[l],
                          lru_wx[l], lru_bx[l], lru_lambda[l])
        y_c = gated_deltanet_mixer(c_q, c_k, c_v, c_z, c_beta, c_alpha, gdn_conv_w[l],
                                   gdn_a_log[l], gdn_dt_bias[l], gdn_norm_g[l])
        y_d = pool_mixer(d_x, pool_w[l], pool_scale[l])
        ys = jnp.stack([y_a, y_b, y_c, y_d], axis=2)
        br = jnp.einsum('bsgc,gcd->bsgd', ys, w_branch[l])
        gates = jax.nn.sigmoid(gate_pre.astype(F32)).astype(x.dtype).reshape(B, S, N_BRANCH, D_MODEL)
        merged = jnp.sum(gates * br, axis=2)
        x = x + jnp.einsum('bsd,de->bse', merged, w_out[l])
        x = x + 0.5 * swiglu(rms_norm(x, ff2_norm[l]), ff2_wg[l], ff2_wu[l], ff2_wd[l])
    return rms_norm(x, final_norm)
```

```python
import functools
import math

import jax
import jax.numpy as jnp
from jax import lax
from jax.experimental import pallas as pl
from jax.experimental.pallas import tpu as pltpu

F32 = jnp.float32
BF16 = jnp.bfloat16

D_MODEL = 1024
DEPTH = 2
CHUNK = 64
N_BRANCH = 4
BRANCH_W = 512
SGU_BLOCK = 128
SGU_GROUPS = 4
LRU_HEADS = 8
LRU_HD = BRANCH_W // LRU_HEADS
LRU_CONV = 4
LRU_C = 8.0
GDN_HEADS = 4
GDN_DK = 128
GDN_DV = 128
GDN_CONV = 4
POOL_WINDOWS = (2, 4, 8, 16)
POOL_GW = 128
D_FF = 2816
EPS = 1e-6

LANES = 128
SUBLANES = 8
VMEM_LIMIT_BYTES = 56 * 1024 * 1024

FFN_ROWS = 512
MIX_ROWS = 256
TAIL = 16

C_AU, C_AV, C_BG, C_CZ = 0, 512, 1024, 1536
C_CONV = 2048
CONV_W = 5 * BRANCH_W
C_GATE = C_CONV + CONV_W
P_MAIN = C_GATE + N_BRANCH * D_MODEL


def _rms(x, g):
    return x * lax.rsqrt(jnp.mean(x * x, axis=-1, keepdims=True) + EPS) * g


def _dot(a, b):
    return jnp.dot(a.astype(BF16), b.astype(BF16), preferred_element_type=F32)


def _dot_nt(a, b):
    return lax.dot_general(a.astype(BF16), b.astype(BF16), (((1,), (1,)), ((), ())),
                           preferred_element_type=F32)


def _dot_tn(a, b):
    return lax.dot_general(a.astype(BF16), b.astype(BF16), (((0,), (0,)), ((), ())),
                           preferred_element_type=F32)


def _split3(a):
    a1 = a.astype(BF16)
    r1 = a - a1.astype(F32)
    a2 = r1.astype(BF16)
    a3 = (r1 - a2.astype(F32)).astype(BF16)
    return a1, a2, a3


def _dot_exact_rhs(sel, a):
    s = sel.astype(BF16)
    a1, a2, a3 = _split3(a)
    return (jnp.dot(s, a1, preferred_element_type=F32)
            + jnp.dot(s, a2, preferred_element_type=F32)
            + jnp.dot(s, a3, preferred_element_type=F32))


def _gelu(x):
    return 0.5 * x * (1.0 + jnp.tanh(math.sqrt(2.0 / math.pi) * (x + 0.044715 * (x * x * x))))


def _sigmoid(x):
    return 1.0 / (1.0 + jnp.exp(-x))


def _silu(x):
    return x * _sigmoid(x)


def _softplus(x):
    return jnp.maximum(x, 0.0) + jnp.log1p(jnp.exp(-jnp.abs(x)))


def _ffn_kernel(x_ref, g_ref, wg_ref, wu_ref, wd_ref, *rest, final):
    if final:
        fg_ref, o_ref = rest
    else:
        (o_ref,) = rest
    x = x_ref[...]
    h = _rms(x, g_ref[...]).astype(BF16)
    a = jnp.dot(h, wg_ref[...], preferred_element_type=F32)
    b = jnp.dot(h, wu_ref[...], preferred_element_type=F32)
    t = (_silu(a) * b).astype(BF16)
    y = x + 0.5 * jnp.dot(t, wd_ref[...], preferred_element_type=F32)
    if final:
        y = _rms(y, fg_ref[...])
    o_ref[...] = y


def _const_spec(shape):
    nd = len(shape)
    return pl.BlockSpec(shape, lambda *_: (0,) * nd, pipeline_mode=pl.Buffered(1))


def _ffn(x2, g, wg, wu, wd, final_g):
    m, d = x2.shape
    final = final_g is not None
    args = [x2, g.reshape(1, d), wg.astype(BF16), wu.astype(BF16), wd.astype(BF16)]
    in_specs = [pl.BlockSpec((FFN_ROWS, d), lambda i: (i, 0)),
                _const_spec((1, d)), _const_spec((d, D_FF)), _const_spec((d, D_FF)),
                _const_spec((D_FF, d))]
    if final:
        args.append(final_g.reshape(1, d))
        in_specs.append(_const_spec((1, d)))
    return pl.pallas_call(
        functools.partial(_ffn_kernel, final=final),
        grid=(m // FFN_ROWS,),
        in_specs=in_specs,
        out_specs=pl.BlockSpec((FFN_ROWS, d), lambda i: (i, 0)),
        out_shape=jax.ShapeDtypeStruct((m, d), F32),
        compiler_params=pltpu.CompilerParams(dimension_semantics=("arbitrary",),
                                             vmem_limit_bytes=VMEM_LIMIT_BYTES),
        name="ffn_final" if final else "ffn",
    )(*args)


def _lane_bcast(a, col, width):
    return jnp.broadcast_to(a[:, col:col + 1], (a.shape[0], width))


def _unit_lower_inverse(a_strict):
    n = a_strict.shape[0]
    ii = lax.broadcasted_iota(jnp.int32, (n, n), 0)
    jj = lax.broadcasted_iota(jnp.int32, (n, n), 1)
    eye = (ii == jj).astype(F32)
    t = eye - jnp.where((ii ^ jj) == 1, a_strict, 0.0)
    b = 2
    while b < n:
        sh = b.bit_length()
        m = ((ii >> sh) == (jj >> sh)) & ((ii & b) != 0) & ((jj & b) == 0)
        am = jnp.where(m, a_strict, 0.0)
        t = t - _dot(_dot(t, am), t)
        b *= 2
    return t


def _mixer_kernel(x_ref, ng_ref, win_ref, wba_ref, lng_ref, lnb_ref, ws_ref, bs_ref,
                  lcw_ref, lcb_ref, wax_ref, lba_ref, lbx_ref, lam_ref,
                  gcw_ref, alog_ref, dtb_ref, gng_ref, pw_ref, psc_ref, wbr_ref, wout_ref,
                  o_ref, cbuf, state, hcar):
    ts = MIX_ROWS
    t_idx = pl.program_id(1)

    @pl.when(t_idx == 0)
    def _():
        cbuf[0:TAIL, :] = jnp.zeros((TAIL, CONV_W), F32)
        state[...] = jnp.zeros(state.shape, F32)
        hcar[...] = jnp.zeros(hcar.shape, F32)

    x = x_ref[0]
    h = _rms(x, ng_ref[...]).astype(BF16)
    p1 = jnp.dot(h, win_ref[:, 0:C_CONV], preferred_element_type=F32)
    cbuf[TAIL:TAIL + ts, :] = jnp.dot(h, win_ref[:, C_CONV:C_GATE], preferred_element_type=F32)
    ba = jnp.dot(h, wba_ref[...], preferred_element_type=F32)

    def hist(k, c0, c1):
        return cbuf[TAIL - k:TAIL - k + ts, c0:c1]

    u = _gelu(p1[:, C_AU:C_AU + BRANCH_W])
    v = _gelu(p1[:, C_AV:C_AV + BRANCH_W])
    mu = jnp.mean(v, axis=-1, keepdims=True)
    vc = v - mu
    var = jnp.mean(vc * vc, axis=-1, keepdims=True)
    vn = vc * lax.rsqrt(var + EPS) * lng_ref[...] + lnb_ref[...]
    bi = lax.broadcasted_iota(jnp.int32, (SGU_BLOCK, SGU_BLOCK), 0)
    bj = lax.broadcasted_iota(jnp.int32, (SGU_BLOCK, SGU_BLOCK), 1)
    sgu_mask = (bi // CHUNK) >= (bj // CHUNK)
    mixed_rows = []
    for blk in range(ts // SGU_BLOCK):
        r0 = blk * SGU_BLOCK
        cols = []
        for g in range(SGU_GROUPS):
            c0 = g * LANES
            wg_m = jnp.where(sgu_mask, ws_ref[g], 0.0)
            cols.append(_dot(wg_m, vn[r0:r0 + SGU_BLOCK, c0:c0 + LANES]) + bs_ref[:, c0:c0 + LANES])
        mixed_rows.append(jnp.concatenate(cols, axis=1))
    y_a = u * jnp.concatenate(mixed_rows, axis=0)

    xc = lcb_ref[...] + sum(lcw_ref[k:k + 1, :] * hist(LRU_CONV - 1 - k, 0, BRANCH_W)
                            for k in range(LRU_CONV))
    ri = _dot(xc, wax_ref[...])
    r = _sigmoid(ri[:, 0:BRANCH_W] + lba_ref[...])
    ig = _sigmoid(ri[:, BRANCH_W:2 * BRANCH_W] + lbx_ref[...])
    log_a = (-LRU_C) * r * _softplus(-lam_ref[...])
    a_t = jnp.exp(log_a)
    b_t = jnp.sqrt(-jnp.tanh(log_a) * (a_t * a_t + 1.0)) * (ig * xc)
    ng = ts // SUBLANES
    a3 = a_t.reshape(ng, SUBLANES, BRANCH_W)
    b3 = b_t.reshape(ng, SUBLANES, BRANCH_W)
    sub = lax.broadcasted_iota(jnp.int32, (ng, SUBLANES, BRANCH_W), 1)
    s = 1
    while s < SUBLANES:
        a_sh = jnp.where(sub >= s, pltpu.roll(a3, s, 1), 1.0)
        b_sh = jnp.where(sub >= s, pltpu.roll(b3, s, 1), 0.0)
        b3 = b3 + a3 * b_sh
        a3 = a3 * a_sh
        s *= 2
    carry = hcar[0:1, :]
    h_rows = []
    for gi in range(ng):
        hg = b3[gi] + a3[gi] * carry
        h_rows.append(hg)
        carry = hg[SUBLANES - 1:SUBLANES, :]
    hcar[0:1, :] = carry
    y_b = jnp.concatenate(h_rows, axis=0) * _gelu(p1[:, C_BG:C_BG + BRANCH_W])

    row = lax.broadcasted_iota(jnp.int32, (ts, POOL_GW), 0) + t_idx * ts
    pooled = []
    for gi, win in enumerate(POOL_WINDOWS):
        c0 = 4 * BRANCH_W + gi * POOL_GW
        cur = hist(0, c0, c0 + POOL_GW)
        acc = cur
        for k in range(1, win):
            acc = acc + hist(k, c0, c0 + POOL_GW)
        cnt = jnp.minimum(row + 1, win).astype(F32)
        pooled.append(acc / cnt - cur)
    y_d = jnp.concatenate(
        [_dot(pooled[gi], pw_ref[gi]) for gi in range(len(POOL_WINDOWS))], axis=1) * psc_ref[...]

    hk = GDN_HEADS * GDN_DK
    qkv = _silu(sum(gcw_ref[k:k + 1, :] * hist(GDN_CONV - 1 - k, BRANCH_W, BRANCH_W + 3 * hk)
                    for k in range(GDN_CONV)))
    beta_c = _sigmoid(ba)
    g_c = -jnp.exp(alog_ref[...]) * _softplus(ba + dtb_ref[...])
    ti = lax.broadcasted_iota(jnp.int32, (ts, ts), 0)
    tj = lax.broadcasted_iota(jnp.int32, (ts, ts), 1)
    lblk = (ti >= tj) & ((ti // CHUNK) == (tj // CHUNK))
    gcum_c = _dot_exact_rhs(jnp.where(lblk, 1.0, 0.0), g_c)
    gcum_r = gcum_c.T[0:SUBLANES, :]
    ci = lax.broadcasted_iota(jnp.int32, (CHUNK, CHUNK), 0)
    cj = lax.broadcasted_iota(jnp.int32, (CHUNK, CHUNK), 1)
    z_all = p1[:, C_CZ:C_CZ + GDN_HEADS * GDN_DV]
    y_c_heads = []
    for hd in range(GDN_HEADS):
        q = qkv[:, hd * GDN_DK:(hd + 1) * GDN_DK]
        k = qkv[:, hk + hd * GDN_DK:hk + (hd + 1) * GDN_DK]
        vv = qkv[:, 2 * hk + hd * GDN_DV:2 * hk + (hd + 1) * GDN_DV]
        q = q * lax.rsqrt(jnp.sum(q * q, axis=-1, keepdims=True) + EPS) * (GDN_DK ** -0.5)
        k = k * lax.rsqrt(jnp.sum(k * k, axis=-1, keepdims=True) + EPS)
        beta_b = _lane_bcast(beta_c, hd, LANES)
        gc_b = _lane_bcast(gcum_c, GDN_HEADS + hd, LANES)
        eg = jnp.exp(gc_b)
        kb = k * beta_b
        vb = vv * beta_b
        kbe = kb * eg
        qe = q * eg
        s_h = state[hd]
        o_chunks = []
        for c in range(ts // CHUNK):
            r0 = c * CHUNK
            rs = slice(r0, r0 + CHUNK)
            gci = gc_b[rs, 0:CHUNK]
            gcj = jnp.broadcast_to(gcum_r[GDN_HEADS + hd:GDN_HEADS + hd + 1, rs], (CHUNK, CHUNK))
            dec = jnp.where(ci >= cj, jnp.exp(jnp.where(ci >= cj, gci - gcj, 0.0)), 0.0)
            a_mat = jnp.where(ci > cj, _dot_nt(kb[rs], k[rs]) * dec, 0.0)
            attn = _dot_nt(q[rs], k[rs]) * dec
            t_inv = _unit_lower_inverse(a_mat)
            u_c = _dot(t_inv, vb[rs])
            w_c = _dot(t_inv, kbe[rs])
            g_last = gc_b[r0 + CHUNK - 1:r0 + CHUNK, :]
            v_new = u_c - _dot(w_c, s_h)
            o_chunks.append(_dot(qe[rs], s_h) + _dot(attn, v_new))
            k_dec = k[rs] * jnp.exp(g_last - gc_b[rs])
            s_h = s_h * jnp.exp(g_last) + _dot_tn(k_dec, v_new)
        state[hd] = s_h
        o = jnp.concatenate(o_chunks, axis=0)
        o = o * lax.rsqrt(jnp.mean(o * o, axis=-1, keepdims=True) + EPS) * gng_ref[...]
        y_c_heads.append(o * _silu(z_all[:, hd * GDN_DV:(hd + 1) * GDN_DV]))
    y_c = jnp.concatenate(y_c_heads, axis=1)

    merged = None
    for gi, y in enumerate((y_a, y_b, y_c, y_d)):
        gate = _sigmoid(jnp.dot(h, win_ref[:, C_GATE + gi * D_MODEL:C_GATE + (gi + 1) * D_MODEL],
                                preferred_element_type=F32))
        term = gate * _dot(y, wbr_ref[gi])
        merged = term if merged is None else merged + term
    o_ref[0] = x + _dot(merged, wout_ref[...])

    cbuf[0:TAIL, :] = cbuf[ts:ts + TAIL, :]


def _block_diag(w):
    hh, n, _ = w.shape
    eye = jnp.eye(hh, dtype=w.dtype)
    return (eye[:, None, :, None] * w[:, :, None, :]).reshape(hh * n, hh * n)


def _prep_w_in(w):
    s = [0, 512, 1024, 1536, 2048, 2560, 3072, 3584, 4096, 4100, 4104, 4616, 8712]
    sl = lambda i: w[:, s[i]:s[i + 1]]
    main = jnp.concatenate([sl(0), sl(1), sl(3), sl(7), sl(2), sl(4), sl(5), sl(6), sl(10), sl(11)], axis=1)
    ba = jnp.concatenate([sl(8), sl(9), jnp.zeros((w.shape[0], LANES - 2 * GDN_HEADS), w.dtype)], axis=1)
    return main.astype(BF16), ba.astype(BF16)


def _pad_lanes(vec, offset):
    out = jnp.zeros((1, LANES), F32)
    return out.at[0, offset:offset + vec.shape[0]].set(vec)


def _mixer(x, l, mix_norm, w_in, sgu_ln_g, sgu_ln_b, sgu_w, sgu_b, lru_conv_w, lru_conv_b,
           lru_wa, lru_ba, lru_wx, lru_bx, lru_lambda, gdn_conv_w, gdn_a_log, gdn_dt_bias,
           gdn_norm_g, pool_w, pool_scale, w_branch, w_out):
    bsz, seq, d = x.shape
    ts = MIX_ROWS
    w_main, w_ba = _prep_w_in(w_in[l])
    row = lambda a: a.reshape(1, -1)
    args = [
        x, row(mix_norm[l]), w_main, w_ba,
        row(sgu_ln_g[l]), row(sgu_ln_b[l]), sgu_w[l],
        jnp.repeat(sgu_b[l].T, LANES, axis=1),
        lru_conv_w[l], row(lru_conv_b[l]),
        jnp.concatenate([_block_diag(lru_wa[l]), _block_diag(lru_wx[l])], axis=1).astype(BF16),
        row(lru_ba[l]), row(lru_bx[l]), row(lru_lambda[l]),
        gdn_conv_w[l], _pad_lanes(gdn_a_log[l], GDN_HEADS), _pad_lanes(gdn_dt_bias[l], GDN_HEADS),
        row(gdn_norm_g[l]), pool_w[l].astype(BF16), row(pool_scale[l]),
        w_branch[l].astype(BF16), w_out[l].astype(BF16),
    ]
    in_specs = [pl.BlockSpec((1, ts, d), lambda b, t: (b, t, 0))]
    in_specs += [_const_spec(a.shape) for a in args[1:]]
    return pl.pallas_call(
        _mixer_kernel,
        grid=(bsz, seq // ts),
        in_specs=in_specs,
        out_specs=pl.BlockSpec((1, ts, d), lambda b, t: (b, t, 0)),
        out_shape=jax.ShapeDtypeStruct((bsz, seq, d), F32),
        scratch_shapes=[pltpu.VMEM((ts + TAIL, CONV_W), F32),
                        pltpu.VMEM((GDN_HEADS, GDN_DK, GDN_DV), F32),
                        pltpu.VMEM((SUBLANES, BRANCH_W), F32)],
        compiler_params=pltpu.CompilerParams(dimension_semantics=("arbitrary", "arbitrary"),
                                             vmem_limit_bytes=VMEM_LIMIT_BYTES),
        name="mixer",
    )(*args)


def kernel(x, ff1_norm, ff1_wg, ff1_wu, ff1_wd, mix_norm, w_in, sgu_ln_g, sgu_ln_b, sgu_w, sgu_b,
           lru_conv_w, lru_conv_b, lru_wa, lru_ba, lru_wx, lru_bx, lru_lambda, gdn_conv_w,
           gdn_a_log, gdn_dt_bias, gdn_norm_g, pool_w, pool_scale, w_branch, w_out,
           ff2_norm, ff2_wg, ff2_wu, ff2_wd, final_norm):
    bsz, seq, d = x.shape
    assert d == D_MODEL and seq % MIX_ROWS == 0 and (bsz * seq) % FFN_ROWS == 0
    for l in range(DEPTH):
        x = _ffn(x.reshape(bsz * seq, d), ff1_norm[l], ff1_wg[l], ff1_wu[l], ff1_wd[l], None)
        x = _mixer(x.reshape(bsz, seq, d), l, mix_norm, w_in, sgu_ln_g, sgu_ln_b, sgu_w, sgu_b,
                   lru_conv_w, lru_conv_b, lru_wa, lru_ba, lru_wx, lru_bx, lru_lambda,
                   gdn_conv_w, gdn_a_log, gdn_dt_bias, gdn_norm_g, pool_w, pool_scale,
                   w_branch, w_out)
        last = l == DEPTH - 1
        x = _ffn(x.reshape(bsz * seq, d), ff2_norm[l], ff2_wg[l], ff2_wu[l], ff2_wd[l],
                 final_norm if last else None)
    return x.reshape(bsz, seq, d)
```

```python
import functools
import math

import jax
import jax.numpy as jnp
from jax import lax
from jax.experimental import pallas as pl
from jax.experimental.pallas import tpu as pltpu

F32 = jnp.float32
BF16 = jnp.bfloat16

D_MODEL = 1024
DEPTH = 2
CHUNK = 64
N_BRANCH = 4
BRANCH_W = 512
SGU_BLOCK = 128
SGU_GROUPS = 4
LRU_HEADS = 8
LRU_HD = BRANCH_W // LRU_HEADS
LRU_CONV = 4
LRU_C = 8.0
GDN_HEADS = 4
GDN_DK = 128
GDN_DV = 128
GDN_CONV = 4
POOL_WINDOWS = (2, 4, 8, 16)
POOL_GW = 128
D_FF = 2816
EPS = 1e-6

LANES = 128
SUBLANES = 8
VMEM_LIMIT_BYTES = 56 * 1024 * 1024

FFN_ROWS = 512
MIX_ROWS = 256
TAIL = 16

C_AU, C_AV, C_BG, C_CZ = 0, 512, 1024, 1536
C_CONV = 2048
CONV_W = 5 * BRANCH_W
C_GATE = C_CONV + CONV_W
P_MAIN = C_GATE + N_BRANCH * D_MODEL


def _rms(x, g):
    return x * lax.rsqrt(jnp.mean(x * x, axis=-1, keepdims=True) + EPS) * g


def _dot(a, b):
    return jnp.dot(a.astype(BF16), b.astype(BF16), preferred_element_type=F32)


def _dot_nt(a, b):
    return lax.dot_general(a.astype(BF16), b.astype(BF16), (((1,), (1,)), ((), ())),
                           preferred_element_type=F32)


def _dot_tn(a, b):
    return lax.dot_general(a.astype(BF16), b.astype(BF16), (((0,), (0,)), ((), ())),
                           preferred_element_type=F32)


def _split3(a):
    a1 = a.astype(BF16)
    r1 = a - a1.astype(F32)
    a2 = r1.astype(BF16)
    a3 = (r1 - a2.astype(F32)).astype(BF16)
    return a1, a2, a3


def _dot_exact_rhs(sel, a):
    s = sel.astype(BF16)
    a1, a2, a3 = _split3(a)
    return (jnp.dot(s, a1, preferred_element_type=F32)
            + jnp.dot(s, a2, preferred_element_type=F32)
            + jnp.dot(s, a3, preferred_element_type=F32))


def _gelu(x):
    return 0.5 * x * (1.0 + jnp.tanh(math.sqrt(2.0 / math.pi) * (x + 0.044715 * (x * x * x))))


def _sigmoid(x):
    return 1.0 / (1.0 + jnp.exp(-x))


def _silu(x):
    return x * _sigmoid(x)


def _softplus(x):
    return jnp.maximum(x, 0.0) + jnp.log1p(jnp.exp(-jnp.abs(x)))


def _ffn_kernel(x_ref, g_ref, wg_ref, wu_ref, wd_ref, *rest, final):
    if final:
        fg_ref, o_ref = rest
    else:
        (o_ref,) = rest
    x = x_ref[...]
    h = _rms(x, g_ref[...]).astype(BF16)
    a = jnp.dot(h, wg_ref[...], preferred_element_type=F32)
    b = jnp.dot(h, wu_ref[...], preferred_element_type=F32)
    t = (_silu(a) * b).astype(BF16)
    y = x + 0.5 * jnp.dot(t, wd_ref[...], preferred_element_type=F32)
    if final:
        y = _rms(y, fg_ref[...])
    o_ref[...] = y


def _const_spec(shape):
    nd = len(shape)
    return pl.BlockSpec(shape, lambda *_: (0,) * nd, pipeline_mode=pl.Buffered(1))


def _ffn(x2, g, wg, wu, wd, final_g):
    m, d = x2.shape
    final = final_g is not None
    args = [x2, g.reshape(1, d), wg.astype(BF16), wu.astype(BF16), wd.astype(BF16)]
    in_specs = [pl.BlockSpec((FFN_ROWS, d), lambda i: (i, 0)),
                _const_spec((1, d)), _const_spec((d, D_FF)), _const_spec((d, D_FF)),
                _const_spec((D_FF, d))]
    if final:
        args.append(final_g.reshape(1, d))
        in_specs.append(_const_spec((1, d)))
    return pl.pallas_call(
        functools.partial(_ffn_kernel, final=final),
        grid=(m // FFN_ROWS,),
        in_specs=in_specs,
        out_specs=pl.BlockSpec((FFN_ROWS, d), lambda i: (i, 0)),
        out_shape=jax.ShapeDtypeStruct((m, d), F32),
        compiler_params=pltpu.CompilerParams(dimension_semantics=("arbitrary",),
                                             vmem_limit_bytes=VMEM_LIMIT_BYTES),
        name="ffn_final" if final else "ffn",
    )(*args)


def _lane_bcast(a, col, width):
    return jnp.broadcast_to(a[:, col:col + 1], (a.shape[0], width))


def _inverse_masks(n):
    ii = lax.broadcasted_iota(jnp.int32, (n, n), 0)
    jj = lax.broadcasted_iota(jnp.int32, (n, n), 1)
    masks = []
    b = 1
    while b < n:
        sh = b.bit_length()
        masks.append(((ii >> sh) == (jj >> sh)) & ((ii & b) != 0) & ((jj & b) == 0))
        b *= 2
    return (ii == jj), masks


def _mixer_kernel(x_ref, ng_ref, win_ref, wba_ref, lng_ref, lnb_ref, ws_ref, bs_ref,
                  lcw_ref, lcb_ref, wax_ref, lba_ref, lbx_ref, lam_ref,
                  gcw_ref, alog_ref, dtb_ref, gng_ref, pw_ref, psc_ref, wbr_ref, wout_ref,
                  o_ref, cbuf, state, hcar):
    ts = MIX_ROWS
    n_chunks = ts // CHUNK
    t_idx = pl.program_id(1)

    @pl.when(t_idx == 0)
    def _():
        cbuf[0:TAIL, :] = jnp.zeros((TAIL, CONV_W), F32)
        state[...] = jnp.zeros(state.shape, F32)
        hcar[...] = jnp.zeros(hcar.shape, F32)

    x = x_ref[0]
    h = _rms(x, ng_ref[...]).astype(BF16)
    cbuf[TAIL:TAIL + ts, :] = jnp.dot(h, win_ref[:, C_CONV:C_GATE], preferred_element_type=F32)
    ba = jnp.dot(h, wba_ref[...], preferred_element_type=F32)

    def hist(k, c0, c1):
        return cbuf[TAIL - k:TAIL - k + ts, c0:c1]

    res = {}

    def proj(name, c0):
        def run():
            res[name] = jnp.dot(h, win_ref[:, c0:c0 + BRANCH_W], preferred_element_type=F32)
        return run

    def sgu():
        v = _gelu(res["a_v"])
        mu = jnp.mean(v, axis=-1, keepdims=True)
        vc = v - mu
        var = jnp.mean(vc * vc, axis=-1, keepdims=True)
        vn = vc * lax.rsqrt(var + EPS) * lng_ref[...] + lnb_ref[...]
        bi = lax.broadcasted_iota(jnp.int32, (SGU_BLOCK, SGU_BLOCK), 0)
        bj = lax.broadcasted_iota(jnp.int32, (SGU_BLOCK, SGU_BLOCK), 1)
        sgu_mask = (bi // CHUNK) >= (bj // CHUNK)
        w_m = [jnp.where(sgu_mask, ws_ref[g], 0.0).astype(BF16) for g in range(SGU_GROUPS)]
        mixed_rows = []
        for blk in range(ts // SGU_BLOCK):
            r0 = blk * SGU_BLOCK
            cols = []
            for g in range(SGU_GROUPS):
                c0 = g * LANES
                cols.append(_dot(w_m[g], vn[r0:r0 + SGU_BLOCK, c0:c0 + LANES]) + bs_ref[:, c0:c0 + LANES])
            mixed_rows.append(jnp.concatenate(cols, axis=1))
        res["y0"] = _gelu(res["a_u"]) * jnp.concatenate(mixed_rows, axis=0)

    def lru():
        xc = lcb_ref[...] + sum(lcw_ref[k:k + 1, :] * hist(LRU_CONV - 1 - k, 0, BRANCH_W)
                                for k in range(LRU_CONV))
        ri = _dot(xc, wax_ref[...])
        r = _sigmoid(ri[:, 0:BRANCH_W] + lba_ref[...])
        ig = _sigmoid(ri[:, BRANCH_W:2 * BRANCH_W] + lbx_ref[...])
        log_a = (-LRU_C) * r * _softplus(-lam_ref[...])
        a_t = jnp.exp(log_a)
        b_t = jnp.sqrt(-jnp.tanh(log_a) * (a_t * a_t + 1.0)) * (ig * xc)
        ng = ts // SUBLANES
        a3 = a_t.reshape(ng, SUBLANES, BRANCH_W)
        b3 = b_t.reshape(ng, SUBLANES, BRANCH_W)
        sub = lax.broadcasted_iota(jnp.int32, (ng, SUBLANES, BRANCH_W), 1)
        s = 1
        while s < SUBLANES:
            a_sh = jnp.where(sub >= s, pltpu.roll(a3, s, 1), 1.0)
            b_sh = jnp.where(sub >= s, pltpu.roll(b3, s, 1), 0.0)
            b3 = b3 + a3 * b_sh
            a3 = a3 * a_sh
            s *= 2
        carry = hcar[0:1, :]
        h_rows = []
        for gi in range(ng):
            hg = b3[gi] + a3[gi] * carry
            h_rows.append(hg)
            carry = hg[SUBLANES - 1:SUBLANES, :]
        hcar[0:1, :] = carry
        res["y1"] = jnp.concatenate(h_rows, axis=0) * _gelu(res["b_g"])

    def pool():
        row = lax.broadcasted_iota(jnp.int32, (ts, POOL_GW), 0) + t_idx * ts
        outs = []
        for gi, win in enumerate(POOL_WINDOWS):
            c0 = 4 * BRANCH_W + gi * POOL_GW
            cur = hist(0, c0, c0 + POOL_GW)
            acc = cur
            for k in range(1, win):
                acc = acc + hist(k, c0, c0 + POOL_GW)
            cnt = jnp.minimum(row + 1, win).astype(F32)
            outs.append(_dot(acc / cnt - cur, pw_ref[gi]))
        res["y3"] = jnp.concatenate(outs, axis=1) * psc_ref[...]

    half = D_MODEL // 2

    def gate(gi, part):
        def run():
            c0 = C_GATE + gi * D_MODEL + part * half
            res["gate", gi, part] = _sigmoid(jnp.dot(h, win_ref[:, c0:c0 + half], preferred_element_type=F32))
        return run

    def branch(gi):
        def run():
            g = jnp.concatenate([res["gate", gi, 0], res["gate", gi, 1]], axis=1)
            term = g * _dot(res["y%d" % gi], wbr_ref[gi])
            res["merged"] = term if "merged" not in res else res["merged"] + term
        return run

    fillers = [proj("a_v", C_AV), proj("a_u", C_AU),
               proj("b_g", C_BG), sgu, proj("c_z", C_CZ), lru, pool, gate(0, 0), gate(0, 1), gate(1, 0), None, None,
               gate(1, 1),
               branch(0), gate(3, 0), branch(1), gate(3, 1), gate(2, 0), branch(3), gate(2, 1), None]

    def fill():
        f = fillers.pop(0) if fillers else None
        if f is not None:
            f()

    hk = GDN_HEADS * GDN_DK
    qkv = _silu(sum(gcw_ref[k:k + 1, :] * hist(GDN_CONV - 1 - k, BRANCH_W, BRANCH_W + 3 * hk)
                    for k in range(GDN_CONV)))
    beta_c = _sigmoid(ba)
    g_c = -jnp.exp(alog_ref[...]) * _softplus(ba + dtb_ref[...])
    ti = lax.broadcasted_iota(jnp.int32, (ts, ts), 0)
    tj = lax.broadcasted_iota(jnp.int32, (ts, ts), 1)
    lblk = (ti >= tj) & ((ti // CHUNK) == (tj // CHUNK))
    gcum_c = _dot_exact_rhs(jnp.where(lblk, 1.0, 0.0), g_c)
    gcum_r = gcum_c.T[0:SUBLANES, :]
    fill()
    ci = lax.broadcasted_iota(jnp.int32, (CHUNK, CHUNK), 0)
    cj = lax.broadcasted_iota(jnp.int32, (CHUNK, CHUNK), 1)
    heads = range(GDN_HEADS)
    chunks = range(n_chunks)
    rows = [slice(c * CHUNK, (c + 1) * CHUNK) for c in chunks]
    q, k, kb, vb, kbe, qe, gc_b = [], [], [], [], [], [], []
    for hd in heads:
        q_h = qkv[:, hd * GDN_DK:(hd + 1) * GDN_DK]
        k_h = qkv[:, hk + hd * GDN_DK:hk + (hd + 1) * GDN_DK]
        v_h = qkv[:, 2 * hk + hd * GDN_DV:2 * hk + (hd + 1) * GDN_DV]
        q_h = q_h * lax.rsqrt(jnp.sum(q_h * q_h, axis=-1, keepdims=True) + EPS) * (GDN_DK ** -0.5)
        k_h = k_h * lax.rsqrt(jnp.sum(k_h * k_h, axis=-1, keepdims=True) + EPS)
        beta_b = _lane_bcast(beta_c, hd, LANES)
        gcb = _lane_bcast(gcum_c, GDN_HEADS + hd, LANES)
        eg = jnp.exp(gcb)
        q.append(q_h)
        k.append(k_h)
        kb.append(k_h * beta_b)
        vb.append(v_h * beta_b)
        kbe.append(k_h * beta_b * eg)
        qe.append(q_h * eg)
        gc_b.append(gcb)

    pairs = [(hd, c) for c in chunks for hd in heads]
    kk = {p: _dot_nt(kb[p[0]][rows[p[1]]], k[p[0]][rows[p[1]]]) for p in pairs}
    qk = {p: _dot_nt(q[p[0]][rows[p[1]]], k[p[0]][rows[p[1]]]) for p in pairs}
    fill()
    a_mat, attn = {}, {}
    for hd, c in pairs:
        gci = gc_b[hd][rows[c], 0:CHUNK]
        gcj = jnp.broadcast_to(gcum_r[GDN_HEADS + hd:GDN_HEADS + hd + 1, rows[c]], (CHUNK, CHUNK))
        dec = jnp.where(ci >= cj, jnp.exp(jnp.where(ci >= cj, gci - gcj, 0.0)), 0.0)
        a_mat[hd, c] = jnp.where(ci > cj, kk[hd, c] * dec, 0.0)
        attn[hd, c] = qk[hd, c] * dec

    eye_m, inv_masks = _inverse_masks(CHUNK)
    t_inv = {p: jnp.where(eye_m, 1.0, 0.0) - jnp.where(inv_masks[0], a_mat[p], 0.0) for p in pairs}
    for m in inv_masks[1:]:
        xm = {p: _dot(t_inv[p], jnp.where(m, a_mat[p], 0.0)) for p in pairs}
        fill()
        t_inv = {p: t_inv[p] - _dot(xm[p], t_inv[p]) for p in pairs}
        fill()
    uw = {(hd, c): _dot(t_inv[hd, c], jnp.concatenate([vb[hd][rows[c]], kbe[hd][rows[c]]], axis=1))
          for hd, c in pairs}
    fill()

    s_h = [state[hd] for hd in heads]
    o_chunks = [[] for _ in heads]
    for c in chunks:
        ws = [_dot(jnp.concatenate([uw[hd, c][:, GDN_DV:], qe[hd][rows[c]]], axis=0), s_h[hd])
              for hd in heads]
        fill()
        for hd in heads:
            r_last = (c + 1) * CHUNK - 1
            g_last = gc_b[hd][r_last:r_last + 1, :]
            v_new = uw[hd, c][:, 0:GDN_DV] - ws[hd][0:CHUNK]
            o_chunks[hd].append(ws[hd][CHUNK:2 * CHUNK] + _dot(attn[hd, c], v_new))
            k_dec = k[hd][rows[c]] * jnp.exp(g_last - gc_b[hd][rows[c]])
            s_h[hd] = s_h[hd] * jnp.exp(g_last) + _dot_tn(k_dec, v_new)
        fill()
    while fillers:
        fill()
    y_c_heads = []
    for hd in heads:
        state[hd] = s_h[hd]
        o = jnp.concatenate(o_chunks[hd], axis=0)
        o = o * lax.rsqrt(jnp.mean(o * o, axis=-1, keepdims=True) + EPS) * gng_ref[...]
        y_c_heads.append(o * _silu(res["c_z"][:, hd * GDN_DV:(hd + 1) * GDN_DV]))
    res["y2"] = jnp.concatenate(y_c_heads, axis=1)
    branch(2)()

    o_ref[0] = x + _dot(res["merged"], wout_ref[...])

    cbuf[0:TAIL, :] = cbuf[ts:ts + TAIL, :]


def _block_diag(w):
    hh, n, _ = w.shape
    out = jnp.zeros((hh * n, hh * n), w.dtype)
    for i in range(hh):
        out = out.at[i * n:(i + 1) * n, i * n:(i + 1) * n].set(w[i])
    return out


def _prep_w_in(w):
    s = [0, 512, 1024, 1536, 2048, 2560, 3072, 3584, 4096, 4100, 4104, 4616, 8712]
    sl = lambda i: w[:, s[i]:s[i + 1]]
    main = jnp.concatenate([sl(0), sl(1), sl(3), sl(7), sl(2), sl(4), sl(5), sl(6), sl(10), sl(11)], axis=1)
    ba = jnp.concatenate([sl(8), sl(9), jnp.zeros((w.shape[0], LANES - 2 * GDN_HEADS), w.dtype)], axis=1)
    return main.astype(BF16), ba.astype(BF16)


def _pad_lanes(vec, offset):
    out = jnp.zeros((1, LANES), F32)
    return out.at[0, offset:offset + vec.shape[0]].set(vec)


def _mixer(x, l, mix_norm, w_in, sgu_ln_g, sgu_ln_b, sgu_w, sgu_b, lru_conv_w, lru_conv_b,
           lru_wa, lru_ba, lru_wx, lru_bx, lru_lambda, gdn_conv_w, gdn_a_log, gdn_dt_bias,
           gdn_norm_g, pool_w, pool_scale, w_branch, w_out):
    bsz, seq, d = x.shape
    ts = MIX_ROWS
    w_main, w_ba = _prep_w_in(w_in[l])
    row = lambda a: a.reshape(1, -1)
    args = [
        x, row(mix_norm[l]), w_main, w_ba,
        row(sgu_ln_g[l]), row(sgu_ln_b[l]), sgu_w[l],
        jnp.repeat(sgu_b[l].T, LANES, axis=1),
        lru_conv_w[l], row(lru_conv_b[l]),
        jnp.concatenate([_block_diag(lru_wa[l]), _block_diag(lru_wx[l])], axis=1).astype(BF16),
        row(lru_ba[l]), row(lru_bx[l]), row(lru_lambda[l]),
        gdn_conv_w[l], _pad_lanes(gdn_a_log[l], GDN_HEADS), _pad_lanes(gdn_dt_bias[l], GDN_HEADS),
        row(gdn_norm_g[l]), pool_w[l].astype(BF16), row(pool_scale[l]),
        w_branch[l].astype(BF16), w_out[l].astype(BF16),
    ]
    in_specs = [pl.BlockSpec((1, ts, d), lambda b, t: (b, t, 0))]
    in_specs += [_const_spec(a.shape) for a in args[1:]]
    return pl.pallas_call(
        _mixer_kernel,
        grid=(bsz, seq // ts),
        in_specs=in_specs,
        out_specs=pl.BlockSpec((1, ts, d), lambda b, t: (b, t, 0)),
        out_shape=jax.ShapeDtypeStruct((bsz, seq, d), F32),
        scratch_shapes=[pltpu.VMEM((ts + TAIL, CONV_W), F32),
                        pltpu.VMEM((GDN_HEADS, GDN_DK, GDN_DV), F32),
                        pltpu.VMEM((SUBLANES, BRANCH_W), F32)],
        compiler_params=pltpu.CompilerParams(dimension_semantics=("arbitrary", "arbitrary"),
                                             vmem_limit_bytes=VMEM_LIMIT_BYTES),
        name="mixer",
    )(*args)


def kernel(x, ff1_norm, ff1_wg, ff1_wu, ff1_wd, mix_norm, w_in, sgu_ln_g, sgu_ln_b, sgu_w, sgu_b,
           lru_conv_w, lru_conv_b, lru_wa, lru_ba, lru_wx, lru_bx, lru_lambda, gdn_conv_w,
           gdn_a_log, gdn_dt_bias, gdn_norm_g, pool_w, pool_scale, w_branch, w_out,
           ff2_norm, ff2_wg, ff2_wu, ff2_wd, final_norm):
    bsz, seq, d = x.shape
    assert d == D_MODEL and seq % MIX_ROWS == 0 and (bsz * seq) % FFN_ROWS == 0
    for l in range(DEPTH):
        x = _ffn(x.reshape(bsz * seq, d), ff1_norm[l], ff1_wg[l], ff1_wu[l], ff1_wd[l], None)
        x = _mixer(x.reshape(bsz, seq, d), l, mix_norm, w_in, sgu_ln_g, sgu_ln_b, sgu_w, sgu_b,
                   lru_conv_w, lru_conv_b, lru_wa, lru_ba, lru_wx, lru_bx, lru_lambda,
                   gdn_conv_w, gdn_a_log, gdn_dt_bias, gdn_norm_g, pool_w, pool_scale,
                   w_branch, w_out)
        last = l == DEPTH - 1
        x = _ffn(x.reshape(bsz * seq, d), ff2_norm[l], ff2_wg[l], ff2_wu[l], ff2_wd[l],
                 final_norm if last else None)
    return x.reshape(bsz, seq, d)
```

```python
import functools
import math

import jax
import jax.numpy as jnp
from jax import lax
from jax.experimental import pallas as pl
from jax.experimental.pallas import tpu as pltpu

F32 = jnp.float32
BF16 = jnp.bfloat16

D_MODEL = 1024
DEPTH = 2
CHUNK = 64
N_BRANCH = 4
BRANCH_W = 512
SGU_BLOCK = 128
SGU_GROUPS = 4
LRU_HEADS = 8
LRU_HD = BRANCH_W // LRU_HEADS
LRU_CONV = 4
LRU_C = 8.0
GDN_HEADS = 4
GDN_DK = 128
GDN_DV = 128
GDN_CONV = 4
POOL_WINDOWS = (2, 4, 8, 16)
POOL_GW = 128
D_FF = 2816
EPS = 1e-6

LANES = 128
SUBLANES = 8
VMEM_LIMIT_BYTES = 56 * 1024 * 1024

FFN_ROWS = 512
MIX_ROWS = 256
TAIL = 16

C_AU, C_AV, C_BX, C_BG, C_QKV, C_CZ = 0, 512, 1024, 1536, 2048, 3584
C_BA = 4096
C_DX = C_BA + LANES
C_GATE = C_DX + BRANCH_W
P_MAIN = C_GATE + N_BRANCH * D_MODEL
CONV_W = 5 * BRANCH_W
H_BX, H_QKV, H_DX = 0, BRANCH_W, 4 * BRANCH_W


def _rms(x, g):
    return x * lax.rsqrt(jnp.mean(x * x, axis=-1, keepdims=True) + EPS) * g


def _dot(a, b):
    return jnp.dot(a.astype(BF16), b.astype(BF16), preferred_element_type=F32)


def _dot_nt(a, b):
    return lax.dot_general(a.astype(BF16), b.astype(BF16), (((1,), (1,)), ((), ())),
                           preferred_element_type=F32)


def _dot_tn(a, b):
    return lax.dot_general(a.astype(BF16), b.astype(BF16), (((0,), (0,)), ((), ())),
                           preferred_element_type=F32)


def _split3(a):
    a1 = a.astype(BF16)
    r1 = a - a1.astype(F32)
    a2 = r1.astype(BF16)
    a3 = (r1 - a2.astype(F32)).astype(BF16)
    return a1, a2, a3


def _dot_exact_rhs(sel, a):
    s = sel.astype(BF16)
    a1, a2, a3 = _split3(a)
    return (jnp.dot(s, a1, preferred_element_type=F32)
            + jnp.dot(s, a2, preferred_element_type=F32)
            + jnp.dot(s, a3, preferred_element_type=F32))


_GELU_C = math.sqrt(2.0 / math.pi)


def _gelu(x):
    hx = 0.5 * x
    return hx + hx * jnp.tanh(x * (_GELU_C + (_GELU_C * 0.044715) * (x * x)))


def _sigmoid(x):
    return 0.5 + 0.5 * jnp.tanh(0.5 * x)


def _silu(x):
    hx = 0.5 * x
    return hx + hx * jnp.tanh(hx)


def _shift_rows(ext, k):
    n = ext.shape[0] // SUBLANES
    e3 = ext.reshape(n, SUBLANES, ext.shape[1])
    rolled = pltpu.roll(e3, k, 1)
    sub = lax.broadcasted_iota(jnp.int32, (n - 1, SUBLANES, ext.shape[1]), 1)
    out = jnp.where(sub < k, rolled[:-1], rolled[1:])
    return out.reshape((n - 1) * SUBLANES, ext.shape[1])


def _softplus(x):
    return jnp.maximum(x, 0.0) + jnp.log1p(jnp.exp(-jnp.abs(x)))


def _ffn_kernel(x_ref, g_ref, wg_ref, wu_ref, wd_ref, *rest, final):
    if final:
        fg_ref, o_ref = rest
    else:
        (o_ref,) = rest
    x = x_ref[...]
    h = _rms(x, g_ref[...]).astype(BF16)
    a = jnp.dot(h, wg_ref[0], preferred_element_type=F32)
    b = jnp.dot(h, wu_ref[0], preferred_element_type=F32)
    t = (_silu(a) * b).astype(BF16)
    y = x + 0.5 * jnp.dot(t, wd_ref[0], preferred_element_type=F32)
    if final:
        y = _rms(y, fg_ref[...])
    o_ref[...] = y


def _const_spec(shape):
    nd = len(shape)
    return pl.BlockSpec(shape, lambda *_: (0,) * nd, pipeline_mode=pl.Buffered(1))


def _layer_spec(arr, l):
    nd = arr.ndim - 1
    return pl.BlockSpec((1,) + arr.shape[1:], lambda *_: (l,) + (0,) * nd, pipeline_mode=pl.Buffered(1))


def _ffn(x2, g, wg, wu, wd, l, final_g):
    m, d = x2.shape
    final = final_g is not None
    args = [x2, g.reshape(1, d), wg, wu, wd]
    in_specs = [pl.BlockSpec((FFN_ROWS, d), lambda i: (i, 0)),
                _const_spec((1, d)), _layer_spec(wg, l), _layer_spec(wu, l), _layer_spec(wd, l)]
    if final:
        args.append(final_g.reshape(1, d))
        in_specs.append(_const_spec((1, d)))
    return pl.pallas_call(
        functools.partial(_ffn_kernel, final=final),
        grid=(m // FFN_ROWS,),
        in_specs=in_specs,
        out_specs=pl.BlockSpec((FFN_ROWS, d), lambda i: (i, 0)),
        out_shape=jax.ShapeDtypeStruct((m, d), F32),
        compiler_params=pltpu.CompilerParams(dimension_semantics=("arbitrary",),
                                             vmem_limit_bytes=VMEM_LIMIT_BYTES),
        name="ffn_final" if final else "ffn",
    )(*args)


def _lane_bcast(a, col, width):
    return jnp.broadcast_to(a[:, col:col + 1], (a.shape[0], width))


def _inverse_masks(n):
    ii = lax.broadcasted_iota(jnp.int32, (n, n), 0)
    jj = lax.broadcasted_iota(jnp.int32, (n, n), 1)
    masks = []
    b = 1
    while b < n:
        sh = b.bit_length()
        masks.append(((ii >> sh) == (jj >> sh)) & ((ii & b) != 0) & ((jj & b) == 0))
        b *= 2
    return (ii == jj), masks


def _mixer_kernel(x_ref, ng_ref, win_ref, lng_ref, lnb_ref, ws_ref, bs_ref,
                  lcw_ref, lcb_ref, wax_ref, lba_ref, lbx_ref, lam_ref,
                  gcw_ref, alog_ref, dtb_ref, gng_ref, pw_ref, psc_ref, wbr_ref, wout_ref,
                  o_ref, cbuf, state, hcar):
    ts = MIX_ROWS
    n_chunks = ts // CHUNK
    t_idx = pl.program_id(1)

    @pl.when(t_idx == 0)
    def _():
        cbuf[0:TAIL, :] = jnp.zeros((TAIL, CONV_W), F32)
        state[...] = jnp.zeros(state.shape, F32)
        hcar[...] = jnp.zeros(hcar.shape, F32)

    x = x_ref[0]
    h = _rms(x, ng_ref[...]).astype(BF16)
    def conv(w_ref, c0, c1):
        ext = cbuf[TAIL - SUBLANES:TAIL + ts, c0:c1]
        taps = w_ref.shape[0]
        y = w_ref[taps - 1:taps, :] * ext[SUBLANES:, :]
        for k in range(1, taps):
            y = y + w_ref[taps - 1 - k:taps - k, :] * _shift_rows(ext, k)
        return y

    res = {}

    def proj(name, c0):
        def run():
            res[name] = jnp.dot(h, win_ref[0, :, c0:c0 + BRANCH_W], preferred_element_type=F32)
        return run

    def sgu():
        v = _gelu(res["a_v"])
        mu = jnp.mean(v, axis=-1, keepdims=True)
        vc = v - mu
        var = jnp.mean(vc * vc, axis=-1, keepdims=True)
        vn = vc * lax.rsqrt(var + EPS) * lng_ref[...] + lnb_ref[...]
        bi = lax.broadcasted_iota(jnp.int32, (SGU_BLOCK, SGU_BLOCK), 0)
        bj = lax.broadcasted_iota(jnp.int32, (SGU_BLOCK, SGU_BLOCK), 1)
        sgu_mask = (bi // CHUNK) >= (bj // CHUNK)
        w_m = [jnp.where(sgu_mask, ws_ref[g], 0.0).astype(BF16) for g in range(SGU_GROUPS)]
        mixed_rows = []
        for blk in range(ts // SGU_BLOCK):
            r0 = blk * SGU_BLOCK
            cols = []
            for g in range(SGU_GROUPS):
                c0 = g * LANES
                cols.append(_dot(w_m[g], vn[r0:r0 + SGU_BLOCK, c0:c0 + LANES]) + bs_ref[:, c0:c0 + LANES])
            mixed_rows.append(jnp.concatenate(cols, axis=1))
        res["y0"] = _gelu(res["a_u"]) * jnp.concatenate(mixed_rows, axis=0)

    def lru():
        xc = conv(lcw_ref, H_BX, H_BX + BRANCH_W) + lcb_ref[...]
        xcb = xc.astype(BF16)
        tw = 2 * LANES
        r_pre, i_pre = [
            jnp.concatenate([jnp.dot(xcb[:, j * tw:(j + 1) * tw], wax_ref[gsel, j],
                                     preferred_element_type=F32) for j in range(BRANCH_W // tw)], axis=1)
            for gsel in (0, 1)]
        r = _sigmoid(r_pre + lba_ref[...])
        ig = _sigmoid(i_pre + lbx_ref[...])
        log_a = (-LRU_C) * r * _softplus(-lam_ref[...])
        a_t = jnp.exp(log_a)
        z = -jnp.tanh(log_a) * (a_t * a_t + 1.0)
        b_t = jnp.where(z > 0.0, z * lax.rsqrt(z), 0.0) * (ig * xc)
        ng = ts // SUBLANES
        a3 = a_t.reshape(ng, SUBLANES, BRANCH_W)
        b3 = b_t.reshape(ng, SUBLANES, BRANCH_W)
        sub = lax.broadcasted_iota(jnp.int32, (ng, SUBLANES, BRANCH_W), 1)
        s = 1
        while s < SUBLANES:
            a_sh = jnp.where(sub >= s, pltpu.roll(a3, s, 1), 1.0)
            b_sh = jnp.where(sub >= s, pltpu.roll(b3, s, 1), 0.0)
            b3 = b3 + a3 * b_sh
            a3 = a3 * a_sh
            s *= 2
        carry = hcar[0:1, :]
        h_rows = []
        for gi in range(ng):
            hg = b3[gi] + a3[gi] * carry
            h_rows.append(hg)
            carry = hg[SUBLANES - 1:SUBLANES, :]
        hcar[0:1, :] = carry
        res["y1"] = jnp.concatenate(h_rows, axis=0) * _gelu(res["b_g"])

    def pool():
        n = (ts + TAIL) // SUBLANES
        sub = lax.broadcasted_iota(jnp.int32, (n, SUBLANES, POOL_GW), 1)

        def shift(a3, k):
            if k == SUBLANES:
                return jnp.concatenate([a3[:1], a3[:-1]], axis=0)
            r = pltpu.roll(a3, k, 1)
            return jnp.where(sub < k, jnp.concatenate([r[:1], r[:-1]], axis=0), r)

        row = lax.broadcasted_iota(jnp.int32, (ts, POOL_GW), 0) + t_idx * ts
        outs = []
        for gi, win in enumerate(POOL_WINDOWS):
            c0 = H_DX + gi * POOL_GW
            acc = cbuf[0:ts + TAIL, c0:c0 + POOL_GW].reshape(n, SUBLANES, POOL_GW)
            k = 1
            while k < win:
                acc = acc + shift(acc, k)
                k *= 2
            acc = acc.reshape(ts + TAIL, POOL_GW)[TAIL:, :]
            cnt = jnp.minimum(row + 1, win).astype(F32)
            outs.append(_dot(acc / cnt - cbuf[TAIL:TAIL + ts, c0:c0 + POOL_GW], pw_ref[gi]))
        res["y3"] = jnp.concatenate(outs, axis=1) * psc_ref[...]

    half = D_MODEL // 2

    def gate(gi, part):
        def run():
            c0 = C_GATE + gi * D_MODEL + part * half
            res["gate", gi, part] = _sigmoid(jnp.dot(h, win_ref[0, :, c0:c0 + half], preferred_element_type=F32))
        return run

    def branch(gi, part):
        def run():
            c0 = part * half
            term = res["gate", gi, part] * _dot(res["y%d" % gi], wbr_ref[0, gi, :, c0:c0 + half])
            res["merged", part] = term if ("merged", part) not in res else res["merged", part] + term
        return run

    proj("a_v", C_AV)()
    proj("a_u", C_AU)()
    for c_in, width, c_hist in ((C_BX, BRANCH_W, H_BX), (C_QKV, 3 * BRANCH_W, H_QKV), (C_DX, BRANCH_W, H_DX)):
        cbuf[TAIL:TAIL + ts, c_hist:c_hist + width] = jnp.dot(h, win_ref[0, :, c_in:c_in + width],
                                                              preferred_element_type=F32)
    ba = jnp.dot(h, win_ref[0, :, C_BA:C_BA + LANES], preferred_element_type=F32)
    proj("b_g", C_BG)()
    lru()

    fillers = [gate(0, 0), gate(0, 1), proj("c_z", C_CZ),
               sgu, gate(1, 0), gate(1, 1), pool, gate(3, 0), gate(3, 1), gate(2, 0), gate(2, 1), None, None,
               None,
               branch(0, 0), branch(0, 1), branch(1, 0), branch(1, 1), branch(3, 0), branch(3, 1), None, None]

    def fill():
        f = fillers.pop(0) if fillers else None
        if f is not None:
            f()

    hk = GDN_HEADS * GDN_DK
    qkv = _silu(conv(gcw_ref, H_QKV, H_QKV + 3 * hk))
    beta_c = _sigmoid(ba)
    g_c = -jnp.exp(alog_ref[...]) * _softplus(ba + dtb_ref[...])
    ti = lax.broadcasted_iota(jnp.int32, (ts, ts), 0)
    tj = lax.broadcasted_iota(jnp.int32, (ts, ts), 1)
    lblk = (ti >= tj) & ((ti // CHUNK) == (tj // CHUNK))
    fill()
    gcum_c = _dot_exact_rhs(jnp.where(lblk, 1.0, 0.0), g_c)
    gcum_r = gcum_c.T[0:SUBLANES, :]
    fill()
    ci = lax.broadcasted_iota(jnp.int32, (CHUNK, CHUNK), 0)
    cj = lax.broadcasted_iota(jnp.int32, (CHUNK, CHUNK), 1)
    heads = range(GDN_HEADS)
    chunks = range(n_chunks)
    rows = [slice(c * CHUNK, (c + 1) * CHUNK) for c in chunks]
    q, k, kb, vb, kbe, qe, gc_b = [], [], [], [], [], [], []
    for hd in heads:
        q_h = qkv[:, hd * GDN_DK:(hd + 1) * GDN_DK]
        k_h = qkv[:, hk + hd * GDN_DK:hk + (hd + 1) * GDN_DK]
        v_h = qkv[:, 2 * hk + hd * GDN_DV:2 * hk + (hd + 1) * GDN_DV]
        q_h = q_h * lax.rsqrt(jnp.sum(q_h * q_h, axis=-1, keepdims=True) + EPS) * (GDN_DK ** -0.5)
        k_h = k_h * lax.rsqrt(jnp.sum(k_h * k_h, axis=-1, keepdims=True) + EPS)
        beta_b = _lane_bcast(beta_c, hd, LANES)
        gcb = _lane_bcast(gcum_c, GDN_HEADS + hd, LANES)
        eg = jnp.exp(gcb)
        q.append(q_h)
        k.append(k_h)
        kb.append(k_h * beta_b)
        vb.append(v_h * beta_b)
        kbe.append(k_h * beta_b * eg)
        qe.append(q_h * eg)
        gc_b.append(gcb)

    pairs = [(hd, c) for c in chunks for hd in heads]
    kk = {p: _dot_nt(kb[p[0]][rows[p[1]]], k[p[0]][rows[p[1]]]) for p in pairs}
    qk = {p: _dot_nt(q[p[0]][rows[p[1]]], k[p[0]][rows[p[1]]]) for p in pairs}
    fill()
    a_mat, attn = {}, {}
    for hd, c in pairs:
        gci = gc_b[hd][rows[c], 0:CHUNK]
        gcj = jnp.broadcast_to(gcum_r[GDN_HEADS + hd:GDN_HEADS + hd + 1, rows[c]], (CHUNK, CHUNK))
        dec = jnp.where(ci >= cj, jnp.exp(jnp.where(ci >= cj, gci - gcj, 0.0)), 0.0)
        a_mat[hd, c] = jnp.where(ci > cj, kk[hd, c] * dec, 0.0)
        attn[hd, c] = qk[hd, c] * dec

    eye_m, inv_masks = _inverse_masks(CHUNK)
    t_inv = {p: jnp.where(eye_m, 1.0, 0.0) - jnp.where(inv_masks[0], a_mat[p], 0.0) for p in pairs}
    for m in inv_masks[1:]:
        xm = {p: _dot(t_inv[p], jnp.where(m, a_mat[p], 0.0)) for p in pairs}
        fill()
        t_inv = {p: t_inv[p] - _dot(xm[p], t_inv[p]) for p in pairs}
        fill()
    uw = {(hd, c): _dot(t_inv[hd, c], jnp.concatenate([vb[hd][rows[c]], kbe[hd][rows[c]]], axis=1))
          for hd, c in pairs}
    fill()

    s_h = [state[hd] for hd in heads]
    o_chunks = [[] for _ in heads]
    for c in chunks:
        ws = [_dot(jnp.concatenate([uw[hd, c][:, GDN_DV:], qe[hd][rows[c]]], axis=0), s_h[hd])
              for hd in heads]
        fill()
        for hd in heads:
            r_last = (c + 1) * CHUNK - 1
            g_last = gc_b[hd][r_last:r_last + 1, :]
            v_new = uw[hd, c][:, 0:GDN_DV] - ws[hd][0:CHUNK]
            o_chunks[hd].append(ws[hd][CHUNK:2 * CHUNK] + _dot(attn[hd, c], v_new))
            k_dec = k[hd][rows[c]] * jnp.exp(g_last - gc_b[hd][rows[c]])
            s_h[hd] = s_h[hd] * jnp.exp(g_last) + _dot_tn(k_dec, v_new)
        fill()
    while fillers:
        fill()
    y_c_heads = []
    for hd in heads:
        state[hd] = s_h[hd]
        o = jnp.concatenate(o_chunks[hd], axis=0)
        o = o * lax.rsqrt(jnp.mean(o * o, axis=-1, keepdims=True) + EPS) * gng_ref[...]
        y_c_heads.append(o * _silu(res["c_z"][:, hd * GDN_DV:(hd + 1) * GDN_DV]))
    res["y2"] = jnp.concatenate(y_c_heads, axis=1)
    branch(2, 0)()
    branch(2, 1)()

    o_ref[0] = x + _dot(jnp.concatenate([res["merged", 0], res["merged", 1]], axis=1), wout_ref[0])

    cbuf[0:TAIL, :] = cbuf[ts:ts + TAIL, :]


def _diag_tiles(w):
    hh, n, _ = w.shape
    per = (2 * LANES) // n
    w5 = w.reshape(hh // per, per, n, 1, n)
    eye = jnp.eye(per, dtype=bool).reshape(1, per, 1, per, 1)
    return jnp.where(eye, w5, 0.0).reshape(hh // per, per * n, per * n)


def _prep_w_in(w):
    n_ba = 2 * GDN_HEADS
    c_ba = 8 * BRANCH_W
    pad = jnp.zeros(w.shape[:2] + (LANES - n_ba,), w.dtype)
    return jnp.concatenate([w[..., :c_ba + n_ba], pad, w[..., c_ba + n_ba:]], axis=-1).astype(BF16)


def _pad_lanes(vec, offset):
    out = jnp.zeros((1, LANES), F32)
    return out.at[0, offset:offset + vec.shape[0]].set(vec)


def _mixer(x, l, big, mix_norm, sgu_ln_g, sgu_ln_b, sgu_w, sgu_b, lru_conv_w, lru_conv_b,
           lru_ba, lru_bx, lru_lambda, gdn_conv_w, gdn_a_log, gdn_dt_bias,
           gdn_norm_g, pool_w, pool_scale):
    bsz, seq, d = x.shape
    ts = MIX_ROWS
    row = lambda a: a.reshape(1, -1)
    args = [
        x, row(mix_norm[l]), big["w_in"],
        row(sgu_ln_g[l]), row(sgu_ln_b[l]), sgu_w[l],
        jnp.repeat(sgu_b[l].T, LANES, axis=1),
        lru_conv_w[l], row(lru_conv_b[l]), big["lru_w"][l],
        row(lru_ba[l]), row(lru_bx[l]), row(lru_lambda[l]),
        gdn_conv_w[l], _pad_lanes(gdn_a_log[l], GDN_HEADS), _pad_lanes(gdn_dt_bias[l], GDN_HEADS),
        row(gdn_norm_g[l]), pool_w[l].astype(BF16), row(pool_scale[l]),
        big["w_branch"], big["w_out"],
    ]
    stacked = (2, len(args) - 2, len(args) - 1)
    in_specs = [pl.BlockSpec((1, ts, d), lambda b, t: (b, t, 0))]
    in_specs += [_layer_spec(a, l) if i in stacked else _const_spec(a.shape)
                 for i, a in enumerate(args) if i > 0]
    return pl.pallas_call(
        _mixer_kernel,
        grid=(bsz, seq // ts),
        in_specs=in_specs,
        out_specs=pl.BlockSpec((1, ts, d), lambda b, t: (b, t, 0)),
        out_shape=jax.ShapeDtypeStruct((bsz, seq, d), F32),
        scratch_shapes=[pltpu.VMEM((ts + TAIL, CONV_W), F32),
                        pltpu.VMEM((GDN_HEADS, GDN_DK, GDN_DV), F32),
                        pltpu.VMEM((SUBLANES, BRANCH_W), F32)],
        compiler_params=pltpu.CompilerParams(dimension_semantics=("arbitrary", "arbitrary"),
                                             vmem_limit_bytes=VMEM_LIMIT_BYTES),
        name="mixer",
    )(*args)


def kernel(x, ff1_norm, ff1_wg, ff1_wu, ff1_wd, mix_norm, w_in, sgu_ln_g, sgu_ln_b, sgu_w, sgu_b,
           lru_conv_w, lru_conv_b, lru_wa, lru_ba, lru_wx, lru_bx, lru_lambda, gdn_conv_w,
           gdn_a_log, gdn_dt_bias, gdn_norm_g, pool_w, pool_scale, w_branch, w_out,
           ff2_norm, ff2_wg, ff2_wu, ff2_wd, final_norm):
    bsz, seq, d = x.shape
    assert d == D_MODEL and seq % MIX_ROWS == 0 and (bsz * seq) % FFN_ROWS == 0
    ff1 = [w.astype(BF16) for w in (ff1_wg, ff1_wu, ff1_wd)]
    ff2 = [w.astype(BF16) for w in (ff2_wg, ff2_wu, ff2_wd)]
    big = {
        "w_in": _prep_w_in(w_in),
        "w_branch": w_branch.astype(BF16),
        "w_out": w_out.astype(BF16),
        "lru_w": jnp.stack([jax.vmap(_diag_tiles)(lru_wa), jax.vmap(_diag_tiles)(lru_wx)], axis=1).astype(BF16),
    }
    for l in range(DEPTH):
        x = _ffn(x.reshape(bsz * seq, d), ff1_norm[l], *ff1, l, None)
        x = _mixer(x.reshape(bsz, seq, d), l, big, mix_norm, sgu_ln_g, sgu_ln_b, sgu_w, sgu_b,
                   lru_conv_w, lru_conv_b, lru_ba, lru_bx, lru_lambda,
                   gdn_conv_w, gdn_a_log, gdn_dt_bias, gdn_norm_g, pool_w, pool_scale)
        last = l == DEPTH - 1
        x = _ffn(x.reshape(bsz * seq, d), ff2_norm[l], *ff2, l, final_norm if last else None)
    return x.reshape(bsz, seq, d)
```

```python
import functools
import math

import jax
import jax.numpy as jnp
from jax import lax
from jax.experimental import pallas as pl
from jax.experimental.pallas import tpu as pltpu

F32 = jnp.float32
BF16 = jnp.bfloat16

D_MODEL = 1024
DEPTH = 2
CHUNK = 64
N_BRANCH = 4
BRANCH_W = 512
SGU_BLOCK = 128
SGU_GROUPS = 4
LRU_HEADS = 8
LRU_HD = BRANCH_W // LRU_HEADS
LRU_CONV = 4
LRU_C = 8.0
GDN_HEADS = 4
GDN_DK = 128
GDN_DV = 128
GDN_CONV = 4
POOL_WINDOWS = (2, 4, 8, 16)
POOL_GW = 128
D_FF = 2816
EPS = 1e-6

LANES = 128
SUBLANES = 8
VMEM_LIMIT_BYTES = 56 * 1024 * 1024

FFN_ROWS = 512
MIX_ROWS = 256
TAIL = 16

C_AU, C_AV, C_BX, C_BG, C_QKV, C_CZ = 0, 512, 1024, 1536, 2048, 3584
P_A = 8 * BRANCH_W
CONV_W = 5 * BRANCH_W
H_BX, H_QKV, H_DX = 0, BRANCH_W, 4 * BRANCH_W


def _rms(x, g):
    return x * lax.rsqrt(jnp.mean(x * x, axis=-1, keepdims=True) + EPS) * g


def _dot(a, b):
    return jnp.dot(a.astype(BF16), b.astype(BF16), preferred_element_type=F32)


def _dot_nt(a, b):
    return lax.dot_general(a.astype(BF16), b.astype(BF16), (((1,), (1,)), ((), ())),
                           preferred_element_type=F32)


def _dot_tn(a, b):
    return lax.dot_general(a.astype(BF16), b.astype(BF16), (((0,), (0,)), ((), ())),
                           preferred_element_type=F32)


def _split3(a):
    a1 = a.astype(BF16)
    r1 = a - a1.astype(F32)
    a2 = r1.astype(BF16)
    a3 = (r1 - a2.astype(F32)).astype(BF16)
    return a1, a2, a3


def _dot_exact_rhs(sel, a):
    s = sel.astype(BF16)
    a1, a2, a3 = _split3(a)
    return (jnp.dot(s, a1, preferred_element_type=F32)
            + jnp.dot(s, a2, preferred_element_type=F32)
            + jnp.dot(s, a3, preferred_element_type=F32))


_GELU_C = math.sqrt(2.0 / math.pi)


def _gelu(x):
    hx = 0.5 * x
    return hx + hx * jnp.tanh(x * (_GELU_C + (_GELU_C * 0.044715) * (x * x)))


def _sigmoid(x):
    return 0.5 + 0.5 * jnp.tanh(0.5 * x)


def _silu(x):
    hx = 0.5 * x
    return hx + hx * jnp.tanh(hx)


def _shift_rows(ext, k):
    n = ext.shape[0] // SUBLANES
    e3 = ext.reshape(n, SUBLANES, ext.shape[1])
    rolled = pltpu.roll(e3, k, 1)
    sub = lax.broadcasted_iota(jnp.int32, (n - 1, SUBLANES, ext.shape[1]), 1)
    out = jnp.where(sub < k, rolled[:-1], rolled[1:])
    return out.reshape((n - 1) * SUBLANES, ext.shape[1])


def _softplus(x):
    return jnp.maximum(x, 0.0) + jnp.log1p(jnp.exp(-jnp.abs(x)))


def _ffn_kernel(x_ref, g_ref, wg_ref, wu_ref, wd_ref, *rest, final):
    if final:
        fg_ref, o_ref = rest
    else:
        (o_ref,) = rest
    x = x_ref[...]
    h = _rms(x, g_ref[...]).astype(BF16)
    a = jnp.dot(h, wg_ref[0], preferred_element_type=F32)
    b = jnp.dot(h, wu_ref[0], preferred_element_type=F32)
    t = (_silu(a) * b).astype(BF16)
    y = x + 0.5 * jnp.dot(t, wd_ref[0], preferred_element_type=F32)
    if final:
        y = _rms(y, fg_ref[...])
    o_ref[...] = y


def _const_spec(shape):
    nd = len(shape)
    return pl.BlockSpec(shape, lambda *_: (0,) * nd, pipeline_mode=pl.Buffered(1))


def _layer_spec(arr, l):
    nd = arr.ndim - 1
    return pl.BlockSpec((1,) + arr.shape[1:], lambda *_: (l,) + (0,) * nd, pipeline_mode=pl.Buffered(1))


def _ffn(x2, g, wg, wu, wd, l, final_g):
    m, d = x2.shape
    final = final_g is not None
    args = [x2, g.reshape(1, d), wg, wu, wd]
    in_specs = [pl.BlockSpec((FFN_ROWS, d), lambda i: (i, 0)),
                _const_spec((1, d)), _layer_spec(wg, l), _layer_spec(wu, l), _layer_spec(wd, l)]
    if final:
        args.append(final_g.reshape(1, d))
        in_specs.append(_const_spec((1, d)))
    return pl.pallas_call(
        functools.partial(_ffn_kernel, final=final),
        grid=(m // FFN_ROWS,),
        in_specs=in_specs,
        out_specs=pl.BlockSpec((FFN_ROWS, d), lambda i: (i, 0)),
        out_shape=jax.ShapeDtypeStruct((m, d), F32),
        compiler_params=pltpu.CompilerParams(dimension_semantics=("arbitrary",),
                                             vmem_limit_bytes=VMEM_LIMIT_BYTES),
        name="ffn_final" if final else "ffn",
    )(*args)


def _lane_bcast(a, col, width):
    return jnp.broadcast_to(a[:, col:col + 1], (a.shape[0], width))


def _inverse_masks(n):
    ii = lax.broadcasted_iota(jnp.int32, (n, n), 0)
    jj = lax.broadcasted_iota(jnp.int32, (n, n), 1)
    masks = []
    b = 1
    while b < n:
        sh = b.bit_length()
        masks.append(((ii >> sh) == (jj >> sh)) & ((ii & b) != 0) & ((jj & b) == 0))
        b *= 2
    return (ii == jj), masks


def _mixer_kernel(x_ref, ng_ref, wa_ref, wg_ref, lng_ref, lnb_ref, ws_ref, bs_ref,
                  lcw_ref, lcb_ref, wax_ref, lba_ref, lbx_ref, lam_ref,
                  gcw_ref, alog_ref, dtb_ref, gng_ref, pw_ref, psc_ref, wbr_ref, wout_ref,
                  o_ref, cbuf, state, hcar):
    ts = MIX_ROWS
    n_chunks = ts // CHUNK
    t_idx = pl.program_id(1)

    @pl.when(t_idx == 0)
    def _():
        cbuf[0:TAIL, :] = jnp.zeros((TAIL, CONV_W), F32)
        state[...] = jnp.zeros(state.shape, F32)
        hcar[...] = jnp.zeros(hcar.shape, F32)

    x = x_ref[0]
    h = _rms(x, ng_ref[...]).astype(BF16)
    def conv(w_ref, c0, c1):
        ext = cbuf[TAIL - SUBLANES:TAIL + ts, c0:c1]
        taps = w_ref.shape[0]
        y = w_ref[taps - 1:taps, :] * ext[SUBLANES:, :]
        for k in range(1, taps):
            y = y + w_ref[taps - 1 - k:taps - k, :] * _shift_rows(ext, k)
        return y

    res = {}

    def proj(name, c0):
        def run():
            res[name] = jnp.dot(h, wa_ref[0, :, c0:c0 + BRANCH_W], preferred_element_type=F32)
        return run

    def sgu():
        v = _gelu(res["a_v"])
        mu = jnp.mean(v, axis=-1, keepdims=True)
        vc = v - mu
        var = jnp.mean(vc * vc, axis=-1, keepdims=True)
        vn = vc * lax.rsqrt(var + EPS) * lng_ref[...] + lnb_ref[...]
        bi = lax.broadcasted_iota(jnp.int32, (SGU_BLOCK, SGU_BLOCK), 0)
        bj = lax.broadcasted_iota(jnp.int32, (SGU_BLOCK, SGU_BLOCK), 1)
        sgu_mask = (bi // CHUNK) >= (bj // CHUNK)
        w_m = [jnp.where(sgu_mask, ws_ref[g], 0.0).astype(BF16) for g in range(SGU_GROUPS)]
        mixed_rows = []
        for blk in range(ts // SGU_BLOCK):
            r0 = blk * SGU_BLOCK
            cols = []
            for g in range(SGU_GROUPS):
                c0 = g * LANES
                cols.append(_dot(w_m[g], vn[r0:r0 + SGU_BLOCK, c0:c0 + LANES]) + bs_ref[:, c0:c0 + LANES])
            mixed_rows.append(jnp.concatenate(cols, axis=1))
        res["y0"] = _gelu(res["a_u"]) * jnp.concatenate(mixed_rows, axis=0)

    def lru():
        xc = conv(lcw_ref, H_BX, H_BX + BRANCH_W) + lcb_ref[...]
        xcb = xc.astype(BF16)
        tw = 2 * LANES
        r_pre, i_pre = [
            jnp.concatenate([jnp.dot(xcb[:, j * tw:(j + 1) * tw], wax_ref[gsel, j],
                                     preferred_element_type=F32) for j in range(BRANCH_W // tw)], axis=1)
            for gsel in (0, 1)]
        r = _sigmoid(r_pre + lba_ref[...])
        ig = _sigmoid(i_pre + lbx_ref[...])
        log_a = (-LRU_C) * r * _softplus(-lam_ref[...])
        a_t = jnp.exp(log_a)
        z = -jnp.tanh(log_a) * (a_t * a_t + 1.0)
        b_t = jnp.where(z > 0.0, z * lax.rsqrt(z), 0.0) * (ig * xc)
        ng = ts // SUBLANES
        a3 = a_t.reshape(ng, SUBLANES, BRANCH_W)
        b3 = b_t.reshape(ng, SUBLANES, BRANCH_W)
        sub = lax.broadcasted_iota(jnp.int32, (ng, SUBLANES, BRANCH_W), 1)
        s = 1
        while s < SUBLANES:
            a_sh = jnp.where(sub >= s, pltpu.roll(a3, s, 1), 1.0)
            b_sh = jnp.where(sub >= s, pltpu.roll(b3, s, 1), 0.0)
            b3 = b3 + a3 * b_sh
            a3 = a3 * a_sh
            s *= 2
        carry = hcar[0:1, :]
        h_rows = []
        for gi in range(ng):
            hg = b3[gi] + a3[gi] * carry
            h_rows.append(hg)
            carry = hg[SUBLANES - 1:SUBLANES, :]
        hcar[0:1, :] = carry
        res["y1"] = jnp.concatenate(h_rows, axis=0) * _gelu(res["b_g"])

    def pool():
        n = (ts + TAIL) // SUBLANES
        sub = lax.broadcasted_iota(jnp.int32, (n, SUBLANES, POOL_GW), 1)

        def shift(a3, k):
            if k == SUBLANES:
                return jnp.concatenate([a3[:1], a3[:-1]], axis=0)
            r = pltpu.roll(a3, k, 1)
            return jnp.where(sub < k, jnp.concatenate([r[:1], r[:-1]], axis=0), r)

        row = lax.broadcasted_iota(jnp.int32, (ts, POOL_GW), 0) + t_idx * ts
        outs = []
        for gi, win in enumerate(POOL_WINDOWS):
            c0 = H_DX + gi * POOL_GW
            acc = cbuf[0:ts + TAIL, c0:c0 + POOL_GW].reshape(n, SUBLANES, POOL_GW)
            k = 1
            while k < win:
                acc = acc + shift(acc, k)
                k *= 2
            acc = acc.reshape(ts + TAIL, POOL_GW)[TAIL:, :]
            cnt = jnp.minimum(row + 1, win).astype(F32)
            outs.append(_dot(acc / cnt - cbuf[TAIL:TAIL + ts, c0:c0 + POOL_GW], pw_ref[gi]))
        res["y3"] = jnp.concatenate(outs, axis=1) * psc_ref[...]

    half = D_MODEL // 2

    def gate(gi, part):
        def run():
            c0 = BRANCH_W + gi * D_MODEL + part * half
            res["gate", gi, part] = jnp.tanh(jnp.dot(h, wg_ref[0, :, c0:c0 + half], preferred_element_type=F32))
        return run

    def branch(gi, part):
        def run():
            c0 = part * half
            t_half = _dot(res["y%d" % gi], wbr_ref[0, gi, :, c0:c0 + half])
            term = t_half + t_half * res["gate", gi, part]
            res["merged", part] = term if ("merged", part) not in res else res["merged", part] + term
        return run

    proj("a_v", C_AV)()
    proj("a_u", C_AU)()
    for w_ref, c_in, width, c_hist in ((wa_ref, C_BX, BRANCH_W, H_BX), (wa_ref, C_QKV, 3 * BRANCH_W, H_QKV),
                                       (wg_ref, 0, BRANCH_W, H_DX)):
        cbuf[TAIL:TAIL + ts, c_hist:c_hist + width] = jnp.dot(h, w_ref[0, :, c_in:c_in + width],
                                                              preferred_element_type=F32)
    ba = jnp.dot(h, wa_ref[0, :, P_A:P_A + LANES], preferred_element_type=F32)
    proj("b_g", C_BG)()
    lru()
    gate(0, 0)()
    gate(0, 1)()
    gate(1, 0)()

    fillers = [gate(1, 1), None, proj("c_z", C_CZ),
               sgu, gate(3, 0), pool, gate(3, 1), gate(2, 0), gate(2, 1), None, None, None, None,
               None,
               branch(0, 0), branch(0, 1), branch(1, 0), branch(1, 1), branch(3, 0), branch(3, 1), None, None]

    def fill():
        f = fillers.pop(0) if fillers else None
        if f is not None:
            f()

    hk = GDN_HEADS * GDN_DK
    qkv = _silu(conv(gcw_ref, H_QKV, H_QKV + 3 * hk))
    beta_c = _sigmoid(ba)
    g_c = -jnp.exp(alog_ref[...]) * _softplus(ba + dtb_ref[...])
    ti = lax.broadcasted_iota(jnp.int32, (ts, ts), 0)
    tj = lax.broadcasted_iota(jnp.int32, (ts, ts), 1)
    lblk = (ti >= tj) & ((ti // CHUNK) == (tj // CHUNK))
    fill()
    gcum_c = _dot_exact_rhs(jnp.where(lblk, 1.0, 0.0), g_c)
    gcum_r = gcum_c.T[0:SUBLANES, :]
    fill()
    ci = lax.broadcasted_iota(jnp.int32, (CHUNK, CHUNK), 0)
    cj = lax.broadcasted_iota(jnp.int32, (CHUNK, CHUNK), 1)
    heads = range(GDN_HEADS)
    chunks = range(n_chunks)
    rows = [slice(c * CHUNK, (c + 1) * CHUNK) for c in chunks]
    q, k, kb, vb, kbe, qe, gc_b = [], [], [], [], [], [], []
    for hd in heads:
        q_h = qkv[:, hd * GDN_DK:(hd + 1) * GDN_DK]
        k_h = qkv[:, hk + hd * GDN_DK:hk + (hd + 1) * GDN_DK]
        v_h = qkv[:, 2 * hk + hd * GDN_DV:2 * hk + (hd + 1) * GDN_DV]
        q_h = q_h * lax.rsqrt(jnp.sum(q_h * q_h, axis=-1, keepdims=True) + EPS) * (GDN_DK ** -0.5)
        k_h = k_h * lax.rsqrt(jnp.sum(k_h * k_h, axis=-1, keepdims=True) + EPS)
        beta_b = _lane_bcast(beta_c, hd, LANES)
        gcb = _lane_bcast(gcum_c, GDN_HEADS + hd, LANES)
        eg = jnp.exp(gcb)
        q.append(q_h)
        k.append(k_h)
        kb.append(k_h * beta_b)
        vb.append(v_h * beta_b)
        kbe.append(k_h * beta_b * eg)
        qe.append(q_h * eg)
        gc_b.append(gcb)

    pairs = [(hd, c) for c in chunks for hd in heads]
    kk = {p: _dot_nt(kb[p[0]][rows[p[1]]], k[p[0]][rows[p[1]]]) for p in pairs}
    qk = {p: _dot_nt(q[p[0]][rows[p[1]]], k[p[0]][rows[p[1]]]) for p in pairs}
    fill()
    a_mat, attn = {}, {}
    for hd, c in pairs:
        gci = gc_b[hd][rows[c], 0:CHUNK]
        gcj = jnp.broadcast_to(gcum_r[GDN_HEADS + hd:GDN_HEADS + hd + 1, rows[c]], (CHUNK, CHUNK))
        dec = jnp.where(ci >= cj, jnp.exp(jnp.where(ci >= cj, gci - gcj, 0.0)), 0.0)
        a_mat[hd, c] = jnp.where(ci > cj, kk[hd, c] * dec, 0.0)
        attn[hd, c] = qk[hd, c] * dec

    eye_m, inv_masks = _inverse_masks(CHUNK)
    t_inv = {p: jnp.where(eye_m, 1.0, 0.0) - jnp.where(inv_masks[0], a_mat[p], 0.0) for p in pairs}
    for m in inv_masks[1:]:
        xm = {p: _dot(t_inv[p], jnp.where(m, a_mat[p], 0.0)) for p in pairs}
        fill()
        t_inv = {p: t_inv[p] - _dot(xm[p], t_inv[p]) for p in pairs}
        fill()
    uw = {(hd, c): _dot(t_inv[hd, c], jnp.concatenate([vb[hd][rows[c]], kbe[hd][rows[c]]], axis=1))
          for hd, c in pairs}
    fill()

    s_h = [state[hd] for hd in heads]
    o_chunks = [[] for _ in heads]
    for c in chunks:
        ws = [_dot(jnp.concatenate([uw[hd, c][:, GDN_DV:], qe[hd][rows[c]]], axis=0), s_h[hd])
              for hd in heads]
        fill()
        for hd in heads:
            r_last = (c + 1) * CHUNK - 1
            g_last = gc_b[hd][r_last:r_last + 1, :]
            v_new = uw[hd, c][:, 0:GDN_DV] - ws[hd][0:CHUNK]
            o_chunks[hd].append(ws[hd][CHUNK:2 * CHUNK] + _dot(attn[hd, c], v_new))
            k_dec = k[hd][rows[c]] * jnp.exp(g_last - gc_b[hd][rows[c]])
            s_h[hd] = s_h[hd] * jnp.exp(g_last) + _dot_tn(k_dec, v_new)
        fill()
    while fillers:
        fill()
    y_c_heads = []
    for hd in heads:
        state[hd] = s_h[hd]
        o = jnp.concatenate(o_chunks[hd], axis=0)
        o = o * lax.rsqrt(jnp.mean(o * o, axis=-1, keepdims=True) + EPS) * gng_ref[...]
        y_c_heads.append(o * _silu(res["c_z"][:, hd * GDN_DV:(hd + 1) * GDN_DV]))
    res["y2"] = jnp.concatenate(y_c_heads, axis=1)
    branch(2, 0)()
    branch(2, 1)()

    o_ref[0] = x + _dot(jnp.concatenate([res["merged", 0], res["merged", 1]], axis=1), wout_ref[0, :, 0:D_MODEL])

    cbuf[0:TAIL, :] = cbuf[ts:ts + TAIL, :]


def _diag_tiles(w):
    hh, n, _ = w.shape
    per = (2 * LANES) // n
    w5 = w.reshape(hh // per, per, n, 1, n)
    eye = jnp.eye(per, dtype=bool).reshape(1, per, 1, per, 1)
    return jnp.where(eye, w5, 0.0).reshape(hh // per, per * n, per * n)


def _pad_tile(w):
    return jnp.pad(w, [(0, 0)] * (w.ndim - 1) + [(0, LANES)])


def _prep_w_in(w):
    c_dx = P_A + 2 * GDN_HEADS
    w_a = jnp.pad(w[..., :c_dx].astype(BF16), ((0, 0), (0, 0), (0, LANES - 2 * GDN_HEADS)))
    w_g = jnp.concatenate([w[..., c_dx:c_dx + BRANCH_W], 0.5 * w[..., c_dx + BRANCH_W:]], axis=-1)
    return w_a, _pad_tile(w_g.astype(BF16))


def _pad_lanes(vec, offset):
    out = jnp.zeros((1, LANES), F32)
    return out.at[0, offset:offset + vec.shape[0]].set(vec)


def _mixer(x, l, big, mix_norm, sgu_ln_g, sgu_ln_b, sgu_w, sgu_b, lru_conv_w, lru_conv_b,
           lru_ba, lru_bx, lru_lambda, gdn_conv_w, gdn_a_log, gdn_dt_bias,
           gdn_norm_g, pool_w, pool_scale):
    bsz, seq, d = x.shape
    ts = MIX_ROWS
    row = lambda a: a.reshape(1, -1)
    args = [
        x, row(mix_norm[l]), *big["w_in"],
        row(sgu_ln_g[l]), row(sgu_ln_b[l]), sgu_w[l],
        jnp.repeat(sgu_b[l].T, LANES, axis=1),
        lru_conv_w[l], row(lru_conv_b[l]), big["lru_w"][l],
        row(lru_ba[l]), row(lru_bx[l]), row(lru_lambda[l]),
        gdn_conv_w[l], _pad_lanes(gdn_a_log[l], GDN_HEADS), _pad_lanes(gdn_dt_bias[l], GDN_HEADS),
        row(gdn_norm_g[l]), pool_w[l].astype(BF16), row(pool_scale[l]),
        big["w_branch"], big["w_out"],
    ]
    stacked = (2, 3, len(args) - 2, len(args) - 1)
    in_specs = [pl.BlockSpec((1, ts, d), lambda b, t: (b, t, 0))]
    in_specs += [_layer_spec(a, l) if i in stacked else _const_spec(a.shape)
                 for i, a in enumerate(args) if i > 0]
    return pl.pallas_call(
        _mixer_kernel,
        grid=(bsz, seq // ts),
        in_specs=in_specs,
        out_specs=pl.BlockSpec((1, ts, d), lambda b, t: (b, t, 0)),
        out_shape=jax.ShapeDtypeStruct((bsz, seq, d), F32),
        scratch_shapes=[pltpu.VMEM((ts + TAIL, CONV_W), F32),
                        pltpu.VMEM((GDN_HEADS, GDN_DK, GDN_DV), F32),
                        pltpu.VMEM((SUBLANES, BRANCH_W), F32)],
        compiler_params=pltpu.CompilerParams(dimension_semantics=("arbitrary", "arbitrary"),
                                             vmem_limit_bytes=VMEM_LIMIT_BYTES),
        name="mixer",
    )(*args)


def kernel(x, ff1_norm, ff1_wg, ff1_wu, ff1_wd, mix_norm, w_in, sgu_ln_g, sgu_ln_b, sgu_w, sgu_b,
           lru_conv_w, lru_conv_b, lru_wa, lru_ba, lru_wx, lru_bx, lru_lambda, gdn_conv_w,
           gdn_a_log, gdn_dt_bias, gdn_norm_g, pool_w, pool_scale, w_branch, w_out,
           ff2_norm, ff2_wg, ff2_wu, ff2_wd, final_norm):
    bsz, seq, d = x.shape
    assert d == D_MODEL and seq % MIX_ROWS == 0 and (bsz * seq) % FFN_ROWS == 0
    ff1 = [w.astype(BF16) for w in (ff1_wg, ff1_wu, ff1_wd)]
    ff2 = [w.astype(BF16) for w in (ff2_wg, ff2_wu, ff2_wd)]
    big = {
        "w_in": _prep_w_in(w_in),
        "w_branch": _pad_tile((0.5 * w_branch).astype(BF16)),
        "w_out": _pad_tile(w_out.astype(BF16)),
        "lru_w": jnp.stack([jax.vmap(_diag_tiles)(lru_wa), jax.vmap(_diag_tiles)(lru_wx)], axis=1).astype(BF16),
    }
    for l in range(DEPTH):
        x = _ffn(x.reshape(bsz * seq, d), ff1_norm[l], *ff1, l, None)
        x = _mixer(x.reshape(bsz, seq, d), l, big, mix_norm, sgu_ln_g, sgu_ln_b, sgu_w, sgu_b,
                   lru_conv_w, lru_conv_b, lru_ba, lru_bx, lru_lambda,
                   gdn_conv_w, gdn_a_log, gdn_dt_bias, gdn_norm_g, pool_w, pool_scale)
        last = l == DEPTH - 1
        x = _ffn(x.reshape(bsz * seq, d), ff2_norm[l], *ff2, l, final_norm if last else None)
    return x.reshape(bsz, seq, d)
```

```python
import functools
import math

import jax
import jax.numpy as jnp
from jax import lax
from jax.experimental import pallas as pl
from jax.experimental.pallas import tpu as pltpu

F32 = jnp.float32
BF16 = jnp.bfloat16

D_MODEL = 1024
DEPTH = 2
CHUNK = 64
N_BRANCH = 4
BRANCH_W = 512
SGU_BLOCK = 128
SGU_GROUPS = 4
LRU_HEADS = 8
LRU_HD = BRANCH_W // LRU_HEADS
LRU_CONV = 4
LRU_C = 8.0
GDN_HEADS = 4
GDN_DK = 128
GDN_DV = 128
GDN_CONV = 4
POOL_WINDOWS = (2, 4, 8, 16)
POOL_GW = 128
D_FF = 2816
EPS = 1e-6

LANES = 128
SUBLANES = 8
VMEM_LIMIT_BYTES = 56 * 1024 * 1024

FFN_ROWS = 1024
MIX_ROWS = 256
TAIL = 16

C_AU, C_AV, C_BX, C_BG, C_QKV, C_CZ = 0, 512, 1024, 1536, 2048, 3584
P_A = 8 * BRANCH_W
CONV_W = 5 * BRANCH_W
H_BX, H_QKV, H_DX = 0, BRANCH_W, 4 * BRANCH_W


def _rms(x, g):
    return x * lax.rsqrt(jnp.mean(x * x, axis=-1, keepdims=True) + EPS) * g


def _dot(a, b):
    return jnp.dot(a.astype(BF16), b.astype(BF16), preferred_element_type=F32)


def _dot_nt(a, b):
    return lax.dot_general(a.astype(BF16), b.astype(BF16), (((1,), (1,)), ((), ())),
                           preferred_element_type=F32)


def _dot_tn(a, b):
    return lax.dot_general(a.astype(BF16), b.astype(BF16), (((0,), (0,)), ((), ())),
                           preferred_element_type=F32)


def _split3(a):
    a1 = a.astype(BF16)
    r1 = a - a1.astype(F32)
    a2 = r1.astype(BF16)
    a3 = (r1 - a2.astype(F32)).astype(BF16)
    return a1, a2, a3


def _dot_exact_rhs(sel, a):
    s = sel.astype(BF16)
    a1, a2, a3 = _split3(a)
    return (jnp.dot(s, a1, preferred_element_type=F32)
            + jnp.dot(s, a2, preferred_element_type=F32)
            + jnp.dot(s, a3, preferred_element_type=F32))


_GELU_C = math.sqrt(2.0 / math.pi)


def _gelu(x):
    hx = 0.5 * x
    return hx + hx * jnp.tanh(x * (_GELU_C + (_GELU_C * 0.044715) * (x * x)))


def _sigmoid(x):
    return 0.5 + 0.5 * jnp.tanh(0.5 * x)


def _silu(x):
    hx = 0.5 * x
    return hx + hx * jnp.tanh(hx)


def _shift_rows(ext, k):
    n = ext.shape[0] // SUBLANES
    e3 = ext.reshape(n, SUBLANES, ext.shape[1])
    rolled = pltpu.roll(e3, k, 1)
    sub = lax.broadcasted_iota(jnp.int32, (n - 1, SUBLANES, ext.shape[1]), 1)
    out = jnp.where(sub < k, rolled[:-1], rolled[1:])
    return out.reshape((n - 1) * SUBLANES, ext.shape[1])


def _softplus(x):
    return jnp.maximum(x, 0.0) + jnp.log1p(jnp.exp(-jnp.abs(x)))


def _ffn_kernel(x_ref, g_ref, wg_ref, wu_ref, wd_ref, *rest, final):
    if final:
        fg_ref, o_ref = rest
    else:
        (o_ref,) = rest
    rows = x_ref.shape[0] // 2
    halves = [slice(i * rows, (i + 1) * rows) for i in range(2)]
    xs = [x_ref[r, :] for r in halves]
    hs = [_rms(x, g_ref[...]).astype(BF16) for x in xs]
    ab = [(jnp.dot(h, wg_ref[0], preferred_element_type=F32),
           jnp.dot(h, wu_ref[0], preferred_element_type=F32)) for h in hs]
    ts = [(_silu(a) * b).astype(BF16) for a, b in ab]
    for r, x, t in zip(halves, xs, ts):
        y = x + 0.5 * jnp.dot(t, wd_ref[0], preferred_element_type=F32)
        if final:
            y = _rms(y, fg_ref[...])
        o_ref[r, :] = y


def _const_spec(shape):
    nd = len(shape)
    return pl.BlockSpec(shape, lambda *_: (0,) * nd, pipeline_mode=pl.Buffered(1))


def _layer_spec(arr, l):
    nd = arr.ndim - 1
    return pl.BlockSpec((1,) + arr.shape[1:], lambda *_: (l,) + (0,) * nd, pipeline_mode=pl.Buffered(1))


def _ffn(x2, g, wg, wu, wd, l, final_g):
    m, d = x2.shape
    final = final_g is not None
    args = [x2, g.reshape(1, d), wg, wu, wd]
    in_specs = [pl.BlockSpec((FFN_ROWS, d), lambda i: (i, 0)),
                _const_spec((1, d)), _layer_spec(wg, l), _layer_spec(wu, l), _layer_spec(wd, l)]
    if final:
        args.append(final_g.reshape(1, d))
        in_specs.append(_const_spec((1, d)))
    return pl.pallas_call(
        functools.partial(_ffn_kernel, final=final),
        grid=(m // FFN_ROWS,),
        in_specs=in_specs,
        out_specs=pl.BlockSpec((FFN_ROWS, d), lambda i: (i, 0)),
        out_shape=jax.ShapeDtypeStruct((m, d), F32),
        compiler_params=pltpu.CompilerParams(dimension_semantics=("arbitrary",),
                                             vmem_limit_bytes=VMEM_LIMIT_BYTES),
        name="ffn_final" if final else "ffn",
    )(*args)


def _lane_bcast(a, col, width):
    return jnp.broadcast_to(a[:, col:col + 1], (a.shape[0], width))


def _inverse_masks(n):
    ii = lax.broadcasted_iota(jnp.int32, (n, n), 0)
    jj = lax.broadcasted_iota(jnp.int32, (n, n), 1)
    masks = []
    b = 1
    while b < n:
        sh = b.bit_length()
        masks.append(((ii >> sh) == (jj >> sh)) & ((ii & b) != 0) & ((jj & b) == 0))
        b *= 2
    return (ii == jj), masks


def _mixer_kernel(x_ref, ng_ref, wa_ref, wg_ref, lng_ref, lnb_ref, ws_ref, bs_ref,
                  lcw_ref, lcb_ref, wax_ref, lba_ref, lbx_ref, lam_ref,
                  gcw_ref, alog_ref, dtb_ref, gng_ref, pw_ref, psc_ref, wbr_ref, wout_ref,
                  o_ref, cbuf, state, hcar):
    ts = MIX_ROWS
    n_chunks = ts // CHUNK
    t_idx = pl.program_id(1)

    @pl.when(t_idx == 0)
    def _():
        cbuf[0:TAIL, :] = jnp.zeros((TAIL, CONV_W), F32)
        state[...] = jnp.zeros(state.shape, F32)
        hcar[...] = jnp.zeros(hcar.shape, F32)

    x = x_ref[0]
    h = _rms(x, ng_ref[...]).astype(BF16)

    def conv(w_ref, c0, c1):
        ext = cbuf[TAIL - SUBLANES:TAIL + ts, c0:c1]
        taps = w_ref.shape[0]
        y = w_ref[taps - 1:taps, :] * ext[SUBLANES:, :]
        for k in range(1, taps):
            y = y + w_ref[taps - 1 - k:taps - k, :] * _shift_rows(ext, k)
        return y

    res = {}

    def proj(name, c0):
        def run():
            res[name] = jnp.dot(h, wa_ref[0, :, c0:c0 + BRANCH_W], preferred_element_type=F32)
        return run

    def sgu():
        v = _gelu(res["a_v"])
        mu = jnp.mean(v, axis=-1, keepdims=True)
        vc = v - mu
        var = jnp.mean(vc * vc, axis=-1, keepdims=True)
        vn = vc * lax.rsqrt(var + EPS) * lng_ref[...] + lnb_ref[...]
        bi = lax.broadcasted_iota(jnp.int32, (SGU_BLOCK, SGU_BLOCK), 0)
        bj = lax.broadcasted_iota(jnp.int32, (SGU_BLOCK, SGU_BLOCK), 1)
        sgu_mask = (bi // CHUNK) >= (bj // CHUNK)
        w_m = [jnp.where(sgu_mask, ws_ref[g], 0.0).astype(BF16) for g in range(SGU_GROUPS)]
        mixed_rows = []
        for blk in range(ts // SGU_BLOCK):
            r0 = blk * SGU_BLOCK
            cols = []
            for g in range(SGU_GROUPS):
                c0 = g * LANES
                cols.append(_dot(w_m[g], vn[r0:r0 + SGU_BLOCK, c0:c0 + LANES]) + bs_ref[:, c0:c0 + LANES])
            mixed_rows.append(jnp.concatenate(cols, axis=1))
        res["y0"] = _gelu(res["a_u"]) * jnp.concatenate(mixed_rows, axis=0)

    def lru():
        xc = conv(lcw_ref, H_BX, H_BX + BRANCH_W) + lcb_ref[...]
        xcb = xc.astype(BF16)
        tw = 2 * LANES
        r_pre, i_pre = [
            jnp.concatenate([jnp.dot(xcb[:, j * tw:(j + 1) * tw], wax_ref[gsel, j],
                                     preferred_element_type=F32) for j in range(BRANCH_W // tw)], axis=1)
            for gsel in (0, 1)]
        r = _sigmoid(r_pre + lba_ref[...])
        ig = _sigmoid(i_pre + lbx_ref[...])
        log_a = (-LRU_C) * r * _softplus(-lam_ref[...])
        a_t = jnp.exp(log_a)
        z = -jnp.tanh(log_a) * (a_t * a_t + 1.0)
        b_t = jnp.where(z > 0.0, z * lax.rsqrt(z), 0.0) * (ig * xc)
        ng = ts // SUBLANES
        a3 = a_t.reshape(ng, SUBLANES, BRANCH_W)
        b3 = b_t.reshape(ng, SUBLANES, BRANCH_W)
        sub = lax.broadcasted_iota(jnp.int32, (ng, SUBLANES, BRANCH_W), 1)
        s = 1
        while s < SUBLANES:
            a_sh = jnp.where(sub >= s, pltpu.roll(a3, s, 1), 1.0)
            b_sh = jnp.where(sub >= s, pltpu.roll(b3, s, 1), 0.0)
            b3 = b3 + a3 * b_sh
            a3 = a3 * a_sh
            s *= 2
        carry = hcar[0:1, :]
        h_rows = []
        for gi in range(ng):
            hg = b3[gi] + a3[gi] * carry
            h_rows.append(hg)
            carry = hg[SUBLANES - 1:SUBLANES, :]
        hcar[0:1, :] = carry
        res["y1"] = jnp.concatenate(h_rows, axis=0) * _gelu(res["b_g"])

    def pool():
        n = (ts + TAIL) // SUBLANES
        sub = lax.broadcasted_iota(jnp.int32, (n, SUBLANES, POOL_GW), 1)

        def shift(a3, k):
            if k == SUBLANES:
                return jnp.concatenate([a3[:1], a3[:-1]], axis=0)
            r = pltpu.roll(a3, k, 1)
            return jnp.where(sub < k, jnp.concatenate([r[:1], r[:-1]], axis=0), r)

        row = lax.broadcasted_iota(jnp.int32, (ts, POOL_GW), 0) + t_idx * ts
        outs = []
        for gi, win in enumerate(POOL_WINDOWS):
            c0 = H_DX + gi * POOL_GW
            acc = cbuf[0:ts + TAIL, c0:c0 + POOL_GW].reshape(n, SUBLANES, POOL_GW)
            k = 1
            while k < win:
                acc = acc + shift(acc, k)
                k *= 2
            acc = acc.reshape(ts + TAIL, POOL_GW)[TAIL:, :]
            cnt = jnp.minimum(row + 1, win).astype(F32)
            outs.append(_dot(acc / cnt - cbuf[TAIL:TAIL + ts, c0:c0 + POOL_GW], pw_ref[gi]))
        res["y3"] = jnp.concatenate(outs, axis=1) * psc_ref[...]

    half = D_MODEL // 2

    def gate(gi, part):
        def run():
            c0 = BRANCH_W + gi * D_MODEL + part * half
            res["gate", gi, part] = jnp.tanh(jnp.dot(h, wg_ref[0, :, c0:c0 + half], preferred_element_type=F32))
        return run

    def branch(gi, part):
        def run():
            c0 = part * half
            t_half = _dot(res["y%d" % gi], wbr_ref[0, gi, :, c0:c0 + half])
            term = t_half + t_half * res["gate", gi, part]
            res["merged", part] = term if ("merged", part) not in res else res["merged", part] + term
        return run

    proj("a_v", C_AV)()
    proj("a_u", C_AU)()
    for w_ref, c_in, width, c_hist in ((wa_ref, C_BX, BRANCH_W, H_BX), (wa_ref, C_QKV, 3 * BRANCH_W, H_QKV),
                                       (wg_ref, 0, BRANCH_W, H_DX)):
        cbuf[TAIL:TAIL + ts, c_hist:c_hist + width] = jnp.dot(h, w_ref[0, :, c_in:c_in + width],
                                                              preferred_element_type=F32)
    ba = jnp.dot(h, wa_ref[0, :, P_A:P_A + LANES], preferred_element_type=F32)
    proj("b_g", C_BG)()
    lru()
    gate(0, 0)()
    gate(0, 1)()
    gate(1, 0)()

    fillers = [gate(1, 1), None, proj("c_z", C_CZ),
               sgu, gate(3, 0), pool, gate(3, 1), gate(2, 0), gate(2, 1), None, None, None, None,
               None,
               branch(0, 0), branch(0, 1), branch(1, 0), branch(1, 1), branch(3, 0), branch(3, 1), None, None]

    def fill():
        f = fillers.pop(0) if fillers else None
        if f is not None:
            f()

    hk = GDN_HEADS * GDN_DK
    qkv = _silu(conv(gcw_ref, H_QKV, H_QKV + 3 * hk))
    beta_c = _sigmoid(ba)
    g_c = -jnp.exp(alog_ref[...]) * _softplus(ba + dtb_ref[...])
    ti = lax.broadcasted_iota(jnp.int32, (ts, ts), 0)
    tj = lax.broadcasted_iota(jnp.int32, (ts, ts), 1)
    lblk = (ti >= tj) & ((ti // CHUNK) == (tj // CHUNK))
    fill()
    gcum_c = _dot_exact_rhs(jnp.where(lblk, 1.0, 0.0), g_c)
    gcum_r = gcum_c.T[0:SUBLANES, :]
    fill()
    ci = lax.broadcasted_iota(jnp.int32, (CHUNK, CHUNK), 0)
    cj = lax.broadcasted_iota(jnp.int32, (CHUNK, CHUNK), 1)
    heads = range(GDN_HEADS)
    chunks = range(n_chunks)
    rows = [slice(c * CHUNK, (c + 1) * CHUNK) for c in chunks]
    q, k, kb, vb, kbe, qe, gc_b = [], [], [], [], [], [], []
    for hd in heads:
        q_h = qkv[:, hd * GDN_DK:(hd + 1) * GDN_DK]
        k_h = qkv[:, hk + hd * GDN_DK:hk + (hd + 1) * GDN_DK]
        v_h = qkv[:, 2 * hk + hd * GDN_DV:2 * hk + (hd + 1) * GDN_DV]
        q_h = q_h * lax.rsqrt(jnp.sum(q_h * q_h, axis=-1, keepdims=True) + EPS) * (GDN_DK ** -0.5)
        k_h = k_h * lax.rsqrt(jnp.sum(k_h * k_h, axis=-1, keepdims=True) + EPS)
        beta_b = _lane_bcast(beta_c, hd, LANES)
        gcb = _lane_bcast(gcum_c, GDN_HEADS + hd, LANES)
        eg = jnp.exp(gcb)
        q.append(q_h)
        k.append(k_h)
        kb.append(k_h * beta_b)
        vb.append(v_h * beta_b)
        kbe.append(k_h * beta_b * eg)
        qe.append(q_h * eg)
        gc_b.append(gcb)

    pairs = [(hd, c) for c in chunks for hd in heads]
    kk = {p: _dot_nt(kb[p[0]][rows[p[1]]], k[p[0]][rows[p[1]]]) for p in pairs}
    qk = {p: _dot_nt(q[p[0]][rows[p[1]]], k[p[0]][rows[p[1]]]) for p in pairs}
    fill()
    a_mat, attn = {}, {}
    for hd, c in pairs:
        gci = gc_b[hd][rows[c], 0:CHUNK]
        gcj = jnp.broadcast_to(gcum_r[GDN_HEADS + hd:GDN_HEADS + hd + 1, rows[c]], (CHUNK, CHUNK))
        dec = jnp.where(ci >= cj, jnp.exp(jnp.where(ci >= cj, gci - gcj, 0.0)), 0.0)
        a_mat[hd, c] = jnp.where(ci > cj, kk[hd, c] * dec, 0.0)
        attn[hd, c] = qk[hd, c] * dec

    eye_m, inv_masks = _inverse_masks(CHUNK)
    t_inv = {p: jnp.where(eye_m, 1.0, 0.0) - jnp.where(inv_masks[0], a_mat[p], 0.0) for p in pairs}
    for m in inv_masks[1:]:
        xm = {p: _dot(t_inv[p], jnp.where(m, a_mat[p], 0.0)) for p in pairs}
        fill()
        t_inv = {p: t_inv[p] - _dot(xm[p], t_inv[p]) for p in pairs}
        fill()
    uw = {(hd, c): _dot(t_inv[hd, c], jnp.concatenate([vb[hd][rows[c]], kbe[hd][rows[c]]], axis=1))
          for hd, c in pairs}
    fill()

    s_h = [state[hd] for hd in heads]
    o_chunks = [[] for _ in heads]
    for c in chunks:
        ws = [_dot(jnp.concatenate([uw[hd, c][:, GDN_DV:], qe[hd][rows[c]]], axis=0), s_h[hd])
              for hd in heads]
        fill()
        for hd in heads:
            r_last = (c + 1) * CHUNK - 1
            g_last = gc_b[hd][r_last:r_last + 1, :]
            v_new = uw[hd, c][:, 0:GDN_DV] - ws[hd][0:CHUNK]
            o_chunks[hd].append(ws[hd][CHUNK:2 * CHUNK] + _dot(attn[hd, c], v_new))
            k_dec = k[hd][rows[c]] * jnp.exp(g_last - gc_b[hd][rows[c]])
            s_h[hd] = s_h[hd] * jnp.exp(g_last) + _dot_tn(k_dec, v_new)
        fill()
    while fillers:
        fill()
    y_c_heads = []
    for hd in heads:
        state[hd] = s_h[hd]
        o = jnp.concatenate(o_chunks[hd], axis=0)
        o = o * lax.rsqrt(jnp.mean(o * o, axis=-1, keepdims=True) + EPS) * gng_ref[...]
        y_c_heads.append(o * _silu(res["c_z"][:, hd * GDN_DV:(hd + 1) * GDN_DV]))
    res["y2"] = jnp.concatenate(y_c_heads, axis=1)
    branch(2, 0)()
    branch(2, 1)()

    o_ref[0] = x + _dot(jnp.concatenate([res["merged", 0], res["merged", 1]], axis=1), wout_ref[0, :, 0:D_MODEL])

    cbuf[0:TAIL, :] = cbuf[ts:ts + TAIL, :]


def _diag_tiles(w):
    hh, n, _ = w.shape
    per = (2 * LANES) // n
    w5 = w.reshape(hh // per, per, n, 1, n)
    eye = jnp.eye(per, dtype=bool).reshape(1, per, 1, per, 1)
    return jnp.where(eye, w5, 0.0).reshape(hh // per, per * n, per * n)


def _pad_tile(w):
    return jnp.pad(w, [(0, 0)] * (w.ndim - 1) + [(0, LANES)])


def _prep_w_in(w):
    c_dx = P_A + 2 * GDN_HEADS
    wt = jnp.swapaxes(w, 1, 2).astype(BF16)
    rows_a = jnp.pad(wt[:, :c_dx], ((0, 0), (0, LANES - 2 * GDN_HEADS), (0, 0)))
    rows_g = jnp.concatenate([wt[:, c_dx:c_dx + BRANCH_W], 0.5 * wt[:, c_dx + BRANCH_W:],
                              jnp.zeros((w.shape[0], LANES, w.shape[1]), BF16)], axis=1)
    return jnp.swapaxes(rows_a, 1, 2), jnp.swapaxes(rows_g, 1, 2)


def _pad_lanes(vec, offset):
    out = jnp.zeros((1, LANES), F32)
    return out.at[0, offset:offset + vec.shape[0]].set(vec)


def _mixer(x, l, big, mix_norm, sgu_ln_g, sgu_ln_b, sgu_w, sgu_b, lru_conv_w, lru_conv_b,
           lru_ba, lru_bx, lru_lambda, gdn_conv_w, gdn_a_log, gdn_dt_bias,
           gdn_norm_g, pool_w, pool_scale):
    bsz, seq, d = x.shape
    ts = MIX_ROWS
    row = lambda a: a.reshape(1, -1)
    args = [
        x, row(mix_norm[l]), *big["w_in"],
        row(sgu_ln_g[l]), row(sgu_ln_b[l]), sgu_w[l],
        jnp.repeat(sgu_b[l].T, LANES, axis=1),
        lru_conv_w[l], row(lru_conv_b[l]), big["lru_w"][l],
        row(lru_ba[l]), row(lru_bx[l]), row(lru_lambda[l]),
        gdn_conv_w[l], _pad_lanes(gdn_a_log[l], GDN_HEADS), _pad_lanes(gdn_dt_bias[l], GDN_HEADS),
        row(gdn_norm_g[l]), pool_w[l].astype(BF16), row(pool_scale[l]),
        big["w_branch"], big["w_out"],
    ]
    stacked = (2, 3, len(args) - 2, len(args) - 1)
    in_specs = [pl.BlockSpec((1, ts, d), lambda b, t: (b, t, 0))]
    in_specs += [_layer_spec(a, l) if i in stacked else _const_spec(a.shape)
                 for i, a in enumerate(args) if i > 0]
    return pl.pallas_call(
        _mixer_kernel,
        grid=(bsz, seq // ts),
        in_specs=in_specs,
        out_specs=pl.BlockSpec((1, ts, d), lambda b, t: (b, t, 0)),
        out_shape=jax.ShapeDtypeStruct((bsz, seq, d), F32),
        scratch_shapes=[pltpu.VMEM((ts + TAIL, CONV_W), F32),
                        pltpu.VMEM((GDN_HEADS, GDN_DK, GDN_DV), F32),
                        pltpu.VMEM((SUBLANES, BRANCH_W), F32)],
        compiler_params=pltpu.CompilerParams(dimension_semantics=("arbitrary", "arbitrary"),
                                             vmem_limit_bytes=VMEM_LIMIT_BYTES),
        name="mixer",
    )(*args)


def kernel(x, ff1_norm, ff1_wg, ff1_wu, ff1_wd, mix_norm, w_in, sgu_ln_g, sgu_ln_b, sgu_w, sgu_b,
           lru_conv_w, lru_conv_b, lru_wa, lru_ba, lru_wx, lru_bx, lru_lambda, gdn_conv_w,
           gdn_a_log, gdn_dt_bias, gdn_norm_g, pool_w, pool_scale, w_branch, w_out,
           ff2_norm, ff2_wg, ff2_wu, ff2_wd, final_norm):
    bsz, seq, d = x.shape
    assert d == D_MODEL and seq % MIX_ROWS == 0 and (bsz * seq) % FFN_ROWS == 0
    ff1 = [w.astype(BF16) for w in (ff1_wg, ff1_wu, ff1_wd)]
    ff2 = [w.astype(BF16) for w in (ff2_wg, ff2_wu, ff2_wd)]
    big = {
        "w_in": _prep_w_in(w_in),
        "w_branch": _pad_tile((0.5 * w_branch).astype(BF16)),
        "w_out": _pad_tile(w_out.astype(BF16)),
        "lru_w": jnp.stack([jax.vmap(_diag_tiles)(lru_wa), jax.vmap(_diag_tiles)(lru_wx)], axis=1).astype(BF16),
    }
    for l in range(DEPTH):
        x = _ffn(x.reshape(bsz * seq, d), ff1_norm[l], *ff1, l, None)
        x = _mixer(x.reshape(bsz, seq, d), l, big, mix_norm, sgu_ln_g, sgu_ln_b, sgu_w, sgu_b,
                   lru_conv_w, lru_conv_b, lru_ba, lru_bx, lru_lambda,
                   gdn_conv_w, gdn_a_log, gdn_dt_bias, gdn_norm_g, pool_w, pool_scale)
        last = l == DEPTH - 1
        x = _ffn(x.reshape(bsz * seq, d), ff2_norm[l], *ff2, l, final_norm if last else None)
    return x.reshape(bsz, seq, d)
```

```python
import functools
import math

import jax
import jax.numpy as jnp
from jax import lax
from jax.experimental import pallas as pl
from jax.experimental.pallas import tpu as pltpu

F32 = jnp.float32
BF16 = jnp.bfloat16

D_MODEL = 1024
DEPTH = 2
CHUNK = 64
N_BRANCH = 4
BRANCH_W = 512
SGU_BLOCK = 128
SGU_GROUPS = 4
LRU_HEADS = 8
LRU_HD = BRANCH_W // LRU_HEADS
LRU_CONV = 4
LRU_C = 8.0
GDN_HEADS = 4
GDN_DK = 128
GDN_DV = 128
GDN_CONV = 4
POOL_WINDOWS = (2, 4, 8, 16)
POOL_GW = 128
D_FF = 2816
EPS = 1e-6

LANES = 128
SUBLANES = 8
VMEM_LIMIT_BYTES = 56 * 1024 * 1024

FFN_ROWS = 1024
MIX_ROWS = 256
TAIL = 16

C_AU, C_AV, C_BX, C_BG, C_QKV, C_CZ = 0, 512, 1024, 1536, 2048, 3584
P_A = 8 * BRANCH_W
CONV_W = 5 * BRANCH_W
H_BX, H_QKV, H_DX = 0, BRANCH_W, 4 * BRANCH_W

_VEC_FIELDS = (("norm", D_MODEL), ("ln_g", BRANCH_W), ("ln_b", BRANCH_W), ("lru_cb", BRANCH_W),
               ("lru_ba", BRANCH_W), ("lru_bx", BRANCH_W), ("lru_lam", BRANCH_W), ("pool_sc", BRANCH_W),
               ("gdn_ng", GDN_DV), ("a_log", LANES), ("dt_b", LANES))
VEC_AT = {}
VEC_W = 0
for _name, _width in _VEC_FIELDS:
    VEC_AT[_name] = (VEC_W, _width)
    VEC_W += _width


def _rms(x, g):
    return x * lax.rsqrt(jnp.mean(x * x, axis=-1, keepdims=True) + EPS) * g


def _dot(a, b):
    return jnp.dot(a.astype(BF16), b.astype(BF16), preferred_element_type=F32)


def _dot_nt(a, b):
    return lax.dot_general(a.astype(BF16), b.astype(BF16), (((1,), (1,)), ((), ())),
                           preferred_element_type=F32)


def _dot_tn(a, b):
    return lax.dot_general(a.astype(BF16), b.astype(BF16), (((0,), (0,)), ((), ())),
                           preferred_element_type=F32)


def _split3(a):
    a1 = a.astype(BF16)
    r1 = a - a1.astype(F32)
    a2 = r1.astype(BF16)
    a3 = (r1 - a2.astype(F32)).astype(BF16)
    return a1, a2, a3


def _dot_exact_rhs(sel, a):
    s = sel.astype(BF16)
    a1, a2, a3 = _split3(a)
    return (jnp.dot(s, a1, preferred_element_type=F32)
            + jnp.dot(s, a2, preferred_element_type=F32)
            + jnp.dot(s, a3, preferred_element_type=F32))


_GELU_C = math.sqrt(2.0 / math.pi)


def _gelu(x):
    hx = 0.5 * x
    return hx + hx * jnp.tanh(x * (_GELU_C + (_GELU_C * 0.044715) * (x * x)))


def _sigmoid(x):
    return 0.5 + 0.5 * jnp.tanh(0.5 * x)


def _silu(x):
    hx = 0.5 * x
    return hx + hx * jnp.tanh(hx)


def _shift_rows(ext, k):
    n = ext.shape[0] // SUBLANES
    e3 = ext.reshape(n, SUBLANES, ext.shape[1])
    rolled = pltpu.roll(e3, k, 1)
    sub = lax.broadcasted_iota(jnp.int32, (n - 1, SUBLANES, ext.shape[1]), 1)
    out = jnp.where(sub < k, rolled[:-1], rolled[1:])
    return out.reshape((n - 1) * SUBLANES, ext.shape[1])


def _softplus(x):
    return jnp.maximum(x, 0.0) + jnp.log1p(jnp.exp(-jnp.abs(x)))


def _ffn_kernel(x_ref, g_ref, wg_ref, wu_ref, wd_ref, *rest, final):
    if final:
        fg_ref, o_ref = rest
    else:
        (o_ref,) = rest
    rows = x_ref.shape[0] // 2
    halves = [slice(i * rows, (i + 1) * rows) for i in range(2)]
    xs = [x_ref[r, :] for r in halves]
    hs = [_rms(x, g_ref[0]).astype(BF16) for x in xs]
    ab = [(jnp.dot(h, wg_ref[0], preferred_element_type=F32),
           jnp.dot(h, wu_ref[0], preferred_element_type=F32)) for h in hs]
    ts = [(_silu(a) * b).astype(BF16) for a, b in ab]
    for r, x, t in zip(halves, xs, ts):
        y = x + 0.5 * jnp.dot(t, wd_ref[0], preferred_element_type=F32)
        if final:
            y = _rms(y, fg_ref[...])
        o_ref[r, :] = y


def _const_spec(shape):
    nd = len(shape)
    return pl.BlockSpec(shape, lambda *_: (0,) * nd, pipeline_mode=pl.Buffered(1))


def _layer_spec(arr, l):
    nd = arr.ndim - 1
    return pl.BlockSpec((1,) + arr.shape[1:], lambda *_: (l,) + (0,) * nd, pipeline_mode=pl.Buffered(1))


def _ffn(x2, g, wg, wu, wd, l, final_g):
    m, d = x2.shape
    final = final_g is not None
    g3 = g.reshape(g.shape[0], 1, d)
    args = [x2, g3, wg, wu, wd]
    in_specs = [pl.BlockSpec((FFN_ROWS, d), lambda i: (i, 0)),
                _layer_spec(g3, l), _layer_spec(wg, l), _layer_spec(wu, l), _layer_spec(wd, l)]
    if final:
        args.append(final_g.reshape(1, d))
        in_specs.append(_const_spec((1, d)))
    return pl.pallas_call(
        functools.partial(_ffn_kernel, final=final),
        grid=(m // FFN_ROWS,),
        in_specs=in_specs,
        out_specs=pl.BlockSpec((FFN_ROWS, d), lambda i: (i, 0)),
        out_shape=jax.ShapeDtypeStruct((m, d), F32),
        compiler_params=pltpu.CompilerParams(dimension_semantics=("arbitrary",),
                                             vmem_limit_bytes=VMEM_LIMIT_BYTES),
        name="ffn_final" if final else "ffn",
    )(*args)


def _lane_bcast(a, col, width):
    return jnp.broadcast_to(a[:, col:col + 1], (a.shape[0], width))


def _inverse_masks(n):
    ii = lax.broadcasted_iota(jnp.int32, (n, n), 0)
    jj = lax.broadcasted_iota(jnp.int32, (n, n), 1)
    masks = []
    b = 1
    while b < n:
        sh = b.bit_length()
        masks.append(((ii >> sh) == (jj >> sh)) & ((ii & b) != 0) & ((jj & b) == 0))
        b *= 2
    return (ii == jj), masks


def _mixer_kernel(x_ref, vec_ref, lblk_ref, wa_ref, wg_ref, ws_ref, bs_ref, cw_ref, wax_ref, pw_ref,
                  wbr_ref, wout_ref, o_ref, cbuf, state, hcar):
    ts = MIX_ROWS
    n_chunks = ts // CHUNK
    t_idx = pl.program_id(1)

    @pl.when(t_idx == 0)
    def _():
        cbuf[0:TAIL, :] = jnp.zeros((TAIL, CONV_W), F32)
        state[...] = jnp.zeros(state.shape, F32)
        hcar[...] = jnp.zeros(hcar.shape, F32)

    def vec(name):
        off, width = VEC_AT[name]
        return vec_ref[0, :, off:off + width]

    x = x_ref[0]
    h = _rms(x, vec("norm")).astype(BF16)

    def conv(c0, c1):
        ext = cbuf[TAIL - SUBLANES:TAIL + ts, c0:c1]
        taps = cw_ref.shape[1]
        y = cw_ref[0, taps - 1:taps, c0:c1] * ext[SUBLANES:, :]
        for k in range(1, taps):
            y = y + cw_ref[0, taps - 1 - k:taps - k, c0:c1] * _shift_rows(ext, k)
        return y

    res = {}

    def proj(name, c0):
        def run():
            res[name] = jnp.dot(h, wa_ref[0, :, c0:c0 + BRANCH_W], preferred_element_type=F32)
        return run

    def sgu():
        v = _gelu(res["a_v"])
        mu = jnp.mean(v, axis=-1, keepdims=True)
        vc = v - mu
        var = jnp.mean(vc * vc, axis=-1, keepdims=True)
        vn = vc * lax.rsqrt(var + EPS) * vec("ln_g") + vec("ln_b")
        bi = lax.broadcasted_iota(jnp.int32, (SGU_BLOCK, SGU_BLOCK), 0)
        bj = lax.broadcasted_iota(jnp.int32, (SGU_BLOCK, SGU_BLOCK), 1)
        sgu_mask = (bi // CHUNK) >= (bj // CHUNK)
        w_m = [jnp.where(sgu_mask, ws_ref[0, g], 0.0).astype(BF16) for g in range(SGU_GROUPS)]
        mixed_rows = []
        for blk in range(ts // SGU_BLOCK):
            r0 = blk * SGU_BLOCK
            cols = []
            for g in range(SGU_GROUPS):
                c0 = g * LANES
                cols.append(_dot(w_m[g], vn[r0:r0 + SGU_BLOCK, c0:c0 + LANES]) + bs_ref[0, :, c0:c0 + LANES])
            mixed_rows.append(jnp.concatenate(cols, axis=1))
        res["y0"] = _gelu(res["a_u"]) * jnp.concatenate(mixed_rows, axis=0)

    def lru():
        xc = conv(H_BX, H_BX + BRANCH_W) + vec("lru_cb")
        xcb = xc.astype(BF16)
        tw = 2 * LANES
        r_pre, i_pre = [
            jnp.concatenate([jnp.dot(xcb[:, j * tw:(j + 1) * tw], wax_ref[0, gsel, j],
                                     preferred_element_type=F32) for j in range(BRANCH_W // tw)], axis=1)
            for gsel in (0, 1)]
        r = _sigmoid(r_pre + vec("lru_ba"))
        ig = _sigmoid(i_pre + vec("lru_bx"))
        log_a = (-LRU_C) * r * _softplus(-vec("lru_lam"))
        a_t = jnp.exp(log_a)
        z = -jnp.tanh(log_a) * (a_t * a_t + 1.0)
        b_t = jnp.where(z > 0.0, z * lax.rsqrt(z), 0.0) * (ig * xc)
        ng = ts // SUBLANES
        a3 = a_t.reshape(ng, SUBLANES, BRANCH_W)
        b3 = b_t.reshape(ng, SUBLANES, BRANCH_W)
        sub = lax.broadcasted_iota(jnp.int32, (ng, SUBLANES, BRANCH_W), 1)
        s = 1
        while s < SUBLANES:
            a_sh = jnp.where(sub >= s, pltpu.roll(a3, s, 1), 1.0)
            b_sh = jnp.where(sub >= s, pltpu.roll(b3, s, 1), 0.0)
            b3 = b3 + a3 * b_sh
            a3 = a3 * a_sh
            s *= 2
        carry = hcar[0:1, :]
        h_rows = []
        for gi in range(ng):
            hg = b3[gi] + a3[gi] * carry
            h_rows.append(hg)
            carry = hg[SUBLANES - 1:SUBLANES, :]
        hcar[0:1, :] = carry
        res["y1"] = jnp.concatenate(h_rows, axis=0) * _gelu(res["b_g"])

    def pool():
        n = (ts + TAIL) // SUBLANES
        sub = lax.broadcasted_iota(jnp.int32, (n, SUBLANES, POOL_GW), 1)

        def shift(a3, k):
            if k == SUBLANES:
                return jnp.concatenate([a3[:1], a3[:-1]], axis=0)
            r = pltpu.roll(a3, k, 1)
            return jnp.where(sub < k, jnp.concatenate([r[:1], r[:-1]], axis=0), r)

        row = lax.broadcasted_iota(jnp.int32, (ts, POOL_GW), 0) + t_idx * ts
        outs = []
        for gi, win in enumerate(POOL_WINDOWS):
            c0 = H_DX + gi * POOL_GW
            acc = cbuf[0:ts + TAIL, c0:c0 + POOL_GW].reshape(n, SUBLANES, POOL_GW)
            k = 1
            while k < win:
                acc = acc + shift(acc, k)
                k *= 2
            acc = acc.reshape(ts + TAIL, POOL_GW)[TAIL:, :]
            cnt = jnp.minimum(row + 1, win).astype(F32)
            outs.append(_dot(acc / cnt - cbuf[TAIL:TAIL + ts, c0:c0 + POOL_GW], pw_ref[0, gi]))
        res["y3"] = jnp.concatenate(outs, axis=1) * vec("pool_sc")

    half = D_MODEL // 2

    def gate(gi, part):
        def run():
            c0 = BRANCH_W + gi * D_MODEL + part * half
            res["gate", gi, part] = jnp.tanh(jnp.dot(h, wg_ref[0, :, c0:c0 + half], preferred_element_type=F32))
        return run

    def branch(gi, part):
        def run():
            c0 = part * half
            t_half = _dot(res["y%d" % gi], wbr_ref[0, gi, :, c0:c0 + half])
            term = t_half + t_half * res["gate", gi, part]
            res["merged", part] = term if ("merged", part) not in res else res["merged", part] + term
        return run

    proj("a_v", C_AV)()
    proj("a_u", C_AU)()
    for w_ref, c_in, width, c_hist in ((wa_ref, C_BX, BRANCH_W, H_BX), (wa_ref, C_QKV, 3 * BRANCH_W, H_QKV),
                                       (wg_ref, 0, BRANCH_W, H_DX)):
        cbuf[TAIL:TAIL + ts, c_hist:c_hist + width] = jnp.dot(h, w_ref[0, :, c_in:c_in + width],
                                                              preferred_element_type=F32)
    ba = jnp.dot(h, wa_ref[0, :, P_A:P_A + LANES], preferred_element_type=F32)
    proj("b_g", C_BG)()
    lru()
    gate(0, 0)()
    gate(0, 1)()
    gate(1, 0)()

    fillers = [gate(1, 1), None, proj("c_z", C_CZ),
               sgu, gate(3, 0), pool, gate(3, 1), gate(2, 0), gate(2, 1), None, None, None, None,
               None,
               branch(0, 0), branch(0, 1), branch(1, 0), branch(1, 1), branch(3, 0), branch(3, 1), None, None]

    def fill():
        f = fillers.pop(0) if fillers else None
        if f is not None:
            f()

    hk = GDN_HEADS * GDN_DK
    qkv = _silu(conv(H_QKV, H_QKV + 3 * hk))
    beta_c = _sigmoid(ba)
    g_c = -jnp.exp(vec("a_log")) * _softplus(ba + vec("dt_b"))
    fill()
    gcum_c = _dot_exact_rhs(lblk_ref[...], g_c)
    gcum_r = gcum_c.T[0:SUBLANES, :]
    fill()
    ci = lax.broadcasted_iota(jnp.int32, (CHUNK, CHUNK), 0)
    cj = lax.broadcasted_iota(jnp.int32, (CHUNK, CHUNK), 1)
    heads = range(GDN_HEADS)
    chunks = range(n_chunks)
    rows = [slice(c * CHUNK, (c + 1) * CHUNK) for c in chunks]
    q, k, kb, vb, kbe, qe, gc_b = [], [], [], [], [], [], []
    for hd in heads:
        q_h = qkv[:, hd * GDN_DK:(hd + 1) * GDN_DK]
        k_h = qkv[:, hk + hd * GDN_DK:hk + (hd + 1) * GDN_DK]
        v_h = qkv[:, 2 * hk + hd * GDN_DV:2 * hk + (hd + 1) * GDN_DV]
        q_h = q_h * lax.rsqrt(jnp.sum(q_h * q_h, axis=-1, keepdims=True) + EPS) * (GDN_DK ** -0.5)
        k_h = k_h * lax.rsqrt(jnp.sum(k_h * k_h, axis=-1, keepdims=True) + EPS)
        beta_b = _lane_bcast(beta_c, hd, LANES)
        gcb = _lane_bcast(gcum_c, GDN_HEADS + hd, LANES)
        eg = jnp.exp(gcb)
        q.append(q_h)
        k.append(k_h)
        kb.append(k_h * beta_b)
        vb.append(v_h * beta_b)
        kbe.append(k_h * beta_b * eg)
        qe.append(q_h * eg)
        gc_b.append(gcb)

    pairs = [(hd, c) for c in chunks for hd in heads]
    kk = {p: _dot_nt(kb[p[0]][rows[p[1]]], k[p[0]][rows[p[1]]]) for p in pairs}
    qk = {p: _dot_nt(q[p[0]][rows[p[1]]], k[p[0]][rows[p[1]]]) for p in pairs}
    fill()
    a_mat, attn = {}, {}
    for hd, c in pairs:
        gci = gc_b[hd][rows[c], 0:CHUNK]
        gcj = jnp.broadcast_to(gcum_r[GDN_HEADS + hd:GDN_HEADS + hd + 1, rows[c]], (CHUNK, CHUNK))
        dec = jnp.where(ci >= cj, jnp.exp(jnp.where(ci >= cj, gci - gcj, 0.0)), 0.0)
        a_mat[hd, c] = jnp.where(ci > cj, kk[hd, c] * dec, 0.0)
        attn[hd, c] = qk[hd, c] * dec

    eye_m, inv_masks = _inverse_masks(CHUNK)
    t_inv = {p: jnp.where(eye_m, 1.0, 0.0) - jnp.where(inv_masks[0], a_mat[p], 0.0) for p in pairs}
    for m in inv_masks[1:]:
        xm = {p: _dot(t_inv[p], jnp.where(m, a_mat[p], 0.0)) for p in pairs}
        fill()
        t_inv = {p: t_inv[p] - _dot(xm[p], t_inv[p]) for p in pairs}
        fill()
    uw = {(hd, c): _dot(t_inv[hd, c], jnp.concatenate([vb[hd][rows[c]], kbe[hd][rows[c]]], axis=1))
          for hd, c in pairs}
    fill()

    s_h = [state[hd] for hd in heads]
    o_chunks = [[] for _ in heads]
    for c in chunks:
        ws = [_dot(jnp.concatenate([uw[hd, c][:, GDN_DV:], qe[hd][rows[c]]], axis=0), s_h[hd])
              for hd in heads]
        fill()
        for hd in heads:
            r_last = (c + 1) * CHUNK - 1
            g_last = gc_b[hd][r_last:r_last + 1, :]
            v_new = uw[hd, c][:, 0:GDN_DV] - ws[hd][0:CHUNK]
            o_chunks[hd].append(ws[hd][CHUNK:2 * CHUNK] + _dot(attn[hd, c], v_new))
            k_dec = k[hd][rows[c]] * jnp.exp(g_last - gc_b[hd][rows[c]])
            s_h[hd] = s_h[hd] * jnp.exp(g_last) + _dot_tn(k_dec, v_new)
        fill()
    while fillers:
        fill()
    y_c_heads = []
    for hd in heads:
        state[hd] = s_h[hd]
        o = jnp.concatenate(o_chunks[hd], axis=0)
        o = o * lax.rsqrt(jnp.mean(o * o, axis=-1, keepdims=True) + EPS) * vec("gdn_ng")
        y_c_heads.append(o * _silu(res["c_z"][:, hd * GDN_DV:(hd + 1) * GDN_DV]))
    res["y2"] = jnp.concatenate(y_c_heads, axis=1)
    branch(2, 0)()
    branch(2, 1)()

    o_ref[0] = x + _dot(jnp.concatenate([res["merged", 0], res["merged", 1]], axis=1), wout_ref[0, :, 0:D_MODEL])

    cbuf[0:TAIL, :] = cbuf[ts:ts + TAIL, :]


def _diag_tiles(w):
    hh, n, _ = w.shape
    per = (2 * LANES) // n
    w5 = w.reshape(hh // per, per, n, 1, n)
    eye = jnp.eye(per, dtype=bool).reshape(1, per, 1, per, 1)
    return jnp.where(eye, w5, 0.0).reshape(hh // per, per * n, per * n)


def _pad_tile(w):
    return jnp.pad(w, [(0, 0)] * (w.ndim - 1) + [(0, LANES)])


def _prep_w_in(w):
    c_dx = P_A + 2 * GDN_HEADS
    wt = jnp.swapaxes(w, 1, 2).astype(BF16)
    rows_a = jnp.pad(wt[:, :c_dx], ((0, 0), (0, LANES - 2 * GDN_HEADS), (0, 0)))
    rows_g = jnp.concatenate([wt[:, c_dx:c_dx + BRANCH_W], 0.5 * wt[:, c_dx + BRANCH_W:],
                              jnp.zeros((w.shape[0], LANES, w.shape[1]), BF16)], axis=1)
    return jnp.swapaxes(rows_a, 1, 2), jnp.swapaxes(rows_g, 1, 2)


def _chunk_lower_ones(ts):
    i = jnp.arange(ts)
    same = (i[:, None] // CHUNK) == (i[None, :] // CHUNK)
    return ((i[:, None] >= i[None, :]) & same).astype(BF16)


def _mixer(x, l, big):
    bsz, seq, d = x.shape
    ts = MIX_ROWS
    stacked = [big[k] for k in ("vecs", "w_a", "w_g", "sgu_w", "sgu_b", "conv_w", "lru_w", "pool_w",
                                "w_branch", "w_out")]
    lblk = _chunk_lower_ones(ts)
    args = [x, stacked[0], lblk] + stacked[1:]
    in_specs = ([pl.BlockSpec((1, ts, d), lambda b, t: (b, t, 0)), _layer_spec(stacked[0], l),
                 _const_spec(lblk.shape)] + [_layer_spec(a, l) for a in stacked[1:]])
    return pl.pallas_call(
        _mixer_kernel,
        grid=(bsz, seq // ts),
        in_specs=in_specs,
        out_specs=pl.BlockSpec((1, ts, d), lambda b, t: (b, t, 0)),
        out_shape=jax.ShapeDtypeStruct((bsz, seq, d), F32),
        scratch_shapes=[pltpu.VMEM((ts + TAIL, CONV_W), F32),
                        pltpu.VMEM((GDN_HEADS, GDN_DK, GDN_DV), F32),
                        pltpu.VMEM((SUBLANES, BRANCH_W), F32)],
        compiler_params=pltpu.CompilerParams(dimension_semantics=("arbitrary", "arbitrary"),
                                             vmem_limit_bytes=VMEM_LIMIT_BYTES),
        name="mixer",
    )(*args)


def kernel(x, ff1_norm, ff1_wg, ff1_wu, ff1_wd, mix_norm, w_in, sgu_ln_g, sgu_ln_b, sgu_w, sgu_b,
           lru_conv_w, lru_conv_b, lru_wa, lru_ba, lru_wx, lru_bx, lru_lambda, gdn_conv_w,
           gdn_a_log, gdn_dt_bias, gdn_norm_g, pool_w, pool_scale, w_branch, w_out,
           ff2_norm, ff2_wg, ff2_wu, ff2_wd, final_norm):
    bsz, seq, d = x.shape
    assert d == D_MODEL and seq % MIX_ROWS == 0 and (bsz * seq) % FFN_ROWS == 0
    ff1 = [w.astype(BF16) for w in (ff1_wg, ff1_wu, ff1_wd)]
    ff2 = [w.astype(BF16) for w in (ff2_wg, ff2_wu, ff2_wd)]
    w_a, w_g = _prep_w_in(w_in)
    head_pad = ((0, 0), (GDN_HEADS, LANES - 2 * GDN_HEADS))
    vec_parts = {"norm": mix_norm, "ln_g": sgu_ln_g, "ln_b": sgu_ln_b, "lru_cb": lru_conv_b, "lru_ba": lru_ba,
                 "lru_bx": lru_bx, "lru_lam": lru_lambda, "pool_sc": pool_scale, "gdn_ng": gdn_norm_g,
                 "a_log": jnp.pad(gdn_a_log, head_pad), "dt_b": jnp.pad(gdn_dt_bias, head_pad)}
    big = {
        "vecs": jnp.concatenate([vec_parts[n] for n, _ in _VEC_FIELDS], axis=1)[:, None, :],
        "w_a": w_a, "w_g": w_g,
        "sgu_w": sgu_w,
        "sgu_b": jnp.repeat(jnp.swapaxes(sgu_b, 1, 2), LANES, axis=2),
        "conv_w": jnp.concatenate([lru_conv_w, gdn_conv_w], axis=2),
        "lru_w": jnp.stack([jax.vmap(_diag_tiles)(lru_wa), jax.vmap(_diag_tiles)(lru_wx)], axis=1).astype(BF16),
        "pool_w": pool_w.astype(BF16),
        "w_branch": _pad_tile((0.5 * w_branch).astype(BF16)),
        "w_out": _pad_tile(w_out.astype(BF16)),
    }
    for l in range(DEPTH):
        x = _ffn(x.reshape(bsz * seq, d), ff1_norm, *ff1, l, None)
        x = _mixer(x.reshape(bsz, seq, d), l, big)
        last = l == DEPTH - 1
        x = _ffn(x.reshape(bsz * seq, d), ff2_norm, *ff2, l, final_norm if last else None)
    return x.reshape(bsz, seq, d)
```
